```python
import math
import jax
import jax.numpy as jnp
from jax import lax
import numpy as np

D_MODEL = 4096
BATCH = 16
SEQ = 256
DEPTH = 4
DEC_BATCH = 8
DEC_SEQ = 2048
PAST_LEN = 256

GRID_W = 64
N_MIXERS = 4
Q_BLOCK = 128
ROPE_BASE = 10000.0
NORM_EPS = 1e-6
NEG_INF = -1e30
D_FF = -(-8 * D_MODEL // (3 * 256)) * 256

N_GQA = (DEPTH + N_MIXERS - 1) // N_MIXERS
N_MLA = (DEPTH + N_MIXERS - 2) // N_MIXERS
N_DIFF = (DEPTH + N_MIXERS - 3) // N_MIXERS
N_NA = DEPTH // N_MIXERS

A_HEADS = 32
A_KV_HEADS = 8
A_HEAD_DIM = 128
B_HEADS = 32
B_Q_RANK = 1024
B_KV_RANK = 512
B_NOPE_DIM = 128
B_ROPE_DIM = 64
B_V_DIM = 128
C_HEADS = 16
C_HEAD_DIM = 128
D_HEADS = 32
D_HEAD_DIM = 128
WIN_R = 8
WIN_C = 16

kernel_name = "hybrid_dit_prefix_step"


def _rms_norm(x, g):
    xf = x.astype(jnp.float32)
    y = xf * lax.rsqrt(jnp.mean(xf * xf, axis=-1, keepdims=True) + NORM_EPS)
    return (y * g.astype(jnp.float32)).astype(x.dtype)


def _axial_rope_angles(n_tokens, rot_dim):
    t = jnp.arange(n_tokens)
    per_axis = rot_dim // 2
    inv_freq = ROPE_BASE ** (-jnp.arange(0, per_axis, 2, dtype=jnp.float32) / per_axis)
    row = (t // GRID_W).astype(jnp.float32)[:, None] * inv_freq
    col = (t % GRID_W).astype(jnp.float32)[:, None] * inv_freq
    ang = jnp.concatenate([row, col], axis=-1)
    return jnp.cos(ang), jnp.sin(ang)


def _apply_rope(x, cos, sin):
    half = x.shape[-1] // 2
    shape = (1, cos.shape[0]) + (1,) * (x.ndim - 3) + (half,)
    c = cos.reshape(shape).astype(x.dtype)
    s = sin.reshape(shape).astype(x.dtype)
    x1, x2 = x[..., :half], x[..., half:]
    return jnp.concatenate([x1 * c - x2 * s, x2 * c + x1 * s], axis=-1)


def _sweep_queries(fn, *qs):
    b, n = qs[0].shape[:2]
    nb = n // Q_BLOCK
    blocks = tuple(jnp.moveaxis(q.reshape((b, nb, Q_BLOCK) + q.shape[2:]), 1, 0) for q in qs)
    out = lax.map(lambda qb: fn(*qb), blocks)
    out = jnp.moveaxis(out, 0, 1)
    return out.reshape((b, n) + out.shape[3:])


def _dense_attention(q, k, v):
    b, n, h, d = q.shape
    hk = k.shape[2]
    qg = q.reshape(b, n, hk, h // hk, d)
    scale = d ** -0.5

    def block(qb):
        s = jnp.einsum("bqhgd,bkhd->bhgqk", qb, k).astype(jnp.float32) * scale
        p = jax.nn.softmax(s, axis=-1).astype(v.dtype)
        return jnp.einsum("bhgqk,bkhe->bqhge", p, v)

    return _sweep_queries(block, qg).reshape(b, n, h, v.shape[-1])


def _gqa_project(h, w_qkv, q_gain, k_gain):
    b, n, _ = h.shape
    nq, nkv = A_HEADS * A_HEAD_DIM, A_KV_HEADS * A_HEAD_DIM
    qkv = h @ w_qkv
    q = qkv[..., :nq].reshape(b, n, A_HEADS, A_HEAD_DIM)
    k = qkv[..., nq:nq + nkv].reshape(b, n, A_KV_HEADS, A_HEAD_DIM)
    v = qkv[..., nq + nkv:].reshape(b, n, A_KV_HEADS, A_HEAD_DIM)
    return _rms_norm(q, q_gain), _rms_norm(k, k_gain), v


def _gqa_context(h, w_qkv, q_gain, k_gain, w_o):
    b, n, _ = h.shape
    q, k, v = _gqa_project(h, w_qkv, q_gain, k_gain)
    o = _dense_attention(q, k, v)
    return o.reshape(b, n, -1) @ w_o, k, v


def _gqa_latent(h, ctx_k, ctx_v, w_qkv, q_gain, k_gain, w_o):
    b, n, _ = h.shape
    q, k, v = _gqa_project(h, w_qkv, q_gain, k_gain)
    cos, sin = _axial_rope_angles(n, A_HEAD_DIM)
    q, k = _apply_rope(q, cos, sin), _apply_rope(k, cos, sin)
    o = _dense_attention(q, jnp.concatenate([ctx_k, k], axis=1), jnp.concatenate([ctx_v, v], axis=1))
    return o.reshape(b, n, -1) @ w_o


def _mla_project(h, w_in, q_gain, kv_gain, w_q_up):
    b, n, _ = h.shape
    a = h @ w_in
    cq = _rms_norm(a[..., :B_Q_RANK], q_gain)
    ckv = _rms_norm(a[..., B_Q_RANK:B_Q_RANK + B_KV_RANK], kv_gain)
    k_pe = a[..., B_Q_RANK + B_KV_RANK:]
    q = (cq @ w_q_up).reshape(b, n, B_HEADS, B_NOPE_DIM + B_ROPE_DIM)
    return q[..., :B_NOPE_DIM], q[..., B_NOPE_DIM:], ckv, k_pe


def _mla_attention(q_nope, q_pe, ckv, k_pe, w_kv_up, w_o):
    b, n = q_nope.shape[:2]
    nk = ckv.shape[1]
    kv = (ckv @ w_kv_up).reshape(b, nk, B_HEADS, B_NOPE_DIM + B_V_DIM)
    k_nope, v = kv[..., :B_NOPE_DIM], kv[..., B_NOPE_DIM:]
    scale = (B_NOPE_DIM + B_ROPE_DIM) ** -0.5

    def block(qn, qp):
        s = jnp.einsum("bqhd,bkhd->bhqk", qn, k_nope) + jnp.einsum("bqhr,bkr->bhqk", qp, k_pe)
        p = jax.nn.softmax(s.astype(jnp.float32) * scale, axis=-1).astype(v.dtype)
        return jnp.einsum("bhqk,bkhe->bqhe", p, v)

    o = _sweep_queries(block, q_nope, q_pe)
    return o.reshape(b, n, -1) @ w_o


def _mla_context(h, w_in, q_gain, kv_gain, w_q_up, w_kv_up, w_o):
    q_nope, q_pe, ckv, k_pe = _mla_project(h, w_in, q_gain, kv_gain, w_q_up)
    return _mla_attention(q_nope, q_pe, ckv, k_pe, w_kv_up, w_o), ckv, k_pe


def _mla_latent(h, ctx_ckv, ctx_kpe, w_in, q_gain, kv_gain, w_q_up, w_kv_up, w_o):
    q_nope, q_pe, ckv, k_pe = _mla_project(h, w_in, q_gain, kv_gain, w_q_up)
    cos, sin = _axial_rope_angles(h.shape[1], B_ROPE_DIM)
    q_pe, k_pe = _apply_rope(q_pe, cos, sin), _apply_rope(k_pe, cos, sin)
    return _mla_attention(q_nope, q_pe, jnp.concatenate([ctx_ckv, ckv], axis=1),
                          jnp.concatenate([ctx_kpe, k_pe], axis=1), w_kv_up, w_o)


def _diff_project(h, w_qkv):
    b, n, _ = h.shape
    w = C_HEADS * 2 * C_HEAD_DIM
    qkv = h @ w_qkv
    q = qkv[..., :w].reshape(b, n, C_HEADS, 2, C_HEAD_DIM)
    k = qkv[..., w:2 * w].reshape(b, n, C_HEADS, 2, C_HEAD_DIM)
    v = qkv[..., 2 * w:].reshape(b, n, C_HEADS, 2 * C_HEAD_DIM)
    return q, k, v


def _diff_attention(q, k, v, lam_q1, lam_k1, lam_q2, lam_k2, subln_g, w_o, lambda_init):
    b, n = q.shape[:2]
    lq1, lk1 = lam_q1.astype(jnp.float32), lam_k1.astype(jnp.float32)
    lq2, lk2 = lam_q2.astype(jnp.float32), lam_k2.astype(jnp.float32)
    lam = jnp.exp(jnp.sum(lq1 * lk1)) - jnp.exp(jnp.sum(lq2 * lk2)) + lambda_init
    scale = C_HEAD_DIM ** -0.5

    def block(qb):
        s = jnp.einsum("bqhjd,bkhjd->bhjqk", qb, k).astype(jnp.float32) * scale
        p = jax.nn.softmax(s, axis=-1)
        wgt = (p[:, :, 0] - lam * p[:, :, 1]).astype(v.dtype)
        return jnp.einsum("bhqk,bkhe->bqhe", wgt, v)

    o = _sweep_queries(block, q)
    o = _rms_norm(o, subln_g) * (1.0 - lambda_init)
    return o.reshape(b, n, -1) @ w_o


def _na_project(h, w_qkv):
    b, n, _ = h.shape
    w = D_HEADS * D_HEAD_DIM
    qkv = h @ w_qkv
    return tuple(qkv[..., i * w:(i + 1) * w].reshape(b, n, D_HEADS, D_HEAD_DIM) for i in range(3))


def _na_context(h, w_qkv, w_o):
    b, n, _ = h.shape
    q, k, v = _na_project(h, w_qkv)
    o = _dense_attention(q, k, v)
    return o.reshape(b, n, -1) @ w_o, k, v


def _na_latent(h, ctx_k, ctx_v, w_qkv, rpb, w_o):
    b, n, _ = h.shape
    rows = n // GRID_W
    wr = min(WIN_R, rows)
    q, k, v = _na_project(h, w_qkv)
    qg = q.reshape(b, rows, GRID_W, D_HEADS, D_HEAD_DIM)
    kg = k.reshape(b, rows, GRID_W, D_HEADS, D_HEAD_DIM)
    vg = v.reshape(b, rows, GRID_W, D_HEADS, D_HEAD_DIM)
    scale = D_HEAD_DIM ** -0.5
    n_ctx = ctx_k.shape[1]
    col = jnp.arange(GRID_W)
    col_start = jnp.clip(col - WIN_C // 2, 0, GRID_W - WIN_C)
    in_win = (col[None, :] >= col_start[:, None]) & (col[None, :] < col_start[:, None] + WIN_C)
    col_off = jnp.clip(col[None, :] - col[:, None] + WIN_C - 1, 0, 2 * WIN_C - 2)
    col_bias = jnp.where(in_win[None, None], rpb[:, :, col_off].astype(jnp.float32), NEG_INF)
    row_start = jnp.clip(jnp.arange(rows) - wr // 2, 0, rows - wr)

    def row_step(r):
        rs = row_start[r]
        qr = lax.dynamic_index_in_dim(qg, r, axis=1, keepdims=False)
        kb = lax.dynamic_slice_in_dim(kg, rs, wr, axis=1)
        vb = lax.dynamic_slice_in_dim(vg, rs, wr, axis=1)
        bias = jnp.take(col_bias, rs - r + WIN_R - 1 + jnp.arange(wr), axis=1)
        s_loc = jnp.einsum("bqhd,brkhd->bhqrk", qr, kb).astype(jnp.float32) * scale
        s_loc = s_loc + jnp.transpose(bias, (0, 2, 1, 3))[None]
        s_ctx = jnp.einsum("bqhd,bkhd->bhqk", qr, ctx_k).astype(jnp.float32) * scale
        s = jnp.concatenate([s_ctx, s_loc.reshape(b, D_HEADS, GRID_W, wr * GRID_W)], axis=-1)
        p = jax.nn.softmax(s, axis=-1).astype(v.dtype)
        p_loc = p[..., n_ctx:].reshape(b, D_HEADS, GRID_W, wr, GRID_W)
        return (jnp.einsum("bhqk,bkhe->bqhe", p[..., :n_ctx], ctx_v)
                + jnp.einsum("bhqrk,brkhe->bqhe", p_loc, vb))

    o = lax.map(row_step, jnp.arange(rows))
    o = jnp.moveaxis(o, 0, 1).reshape(b, n, -1)
    return o @ w_o


def _adaln(cond, w, b):
    m = jax.nn.silu(cond) @ w + b
    m = m.reshape(cond.shape[0], 1, 6, D_MODEL)
    return tuple(m[:, :, i] for i in range(6))


def _modulated_norm(x, g, shift, scale):
    return _rms_norm(x, g) * (1 + scale) + shift


def _gated_post(x, y, g, gate):
    return x + gate * _rms_norm(y, g)


def _swiglu(h, w_gu, w_down):
    gu = h @ w_gu
    return (jax.nn.silu(gu[..., :D_FF]) * gu[..., D_FF:]) @ w_down


def setup_inputs(seed: int = 0) -> dict:
    key = jax.random.key(seed)
    ks = iter(jax.random.split(key, 48))

    def nrm(shape, scale=1.0):
        return jax.random.normal(next(ks), shape, jnp.float32) * scale

    def gain(shape):
        return 1.0 + 0.05 * nrm(shape)

    d = D_MODEL
    return {
        "x_prompt": nrm((BATCH, SEQ, d)),
        "x_sample": nrm((DEC_BATCH, DEC_SEQ, d)),
        "cache_gqa_k": nrm((DEC_BATCH, N_GQA, PAST_LEN, A_KV_HEADS, A_HEAD_DIM)),
        "cache_gqa_v": nrm((DEC_BATCH, N_GQA, PAST_LEN, A_KV_HEADS, A_HEAD_DIM)),
        "cache_mla_ckv": nrm((DEC_BATCH, N_MLA, PAST_LEN, B_KV_RANK)),
        "cache_mla_kpe": nrm((DEC_BATCH, N_MLA, PAST_LEN, B_ROPE_DIM)),
        "cache_diff_k": nrm((DEC_BATCH, N_DIFF, PAST_LEN, C_HEADS, 2, C_HEAD_DIM)),
        "cache_diff_v": nrm((DEC_BATCH, N_DIFF, PAST_LEN, C_HEADS, 2 * C_HEAD_DIM)),
        "cache_na_k": nrm((DEC_BATCH, N_NA, PAST_LEN, D_HEADS, D_HEAD_DIM)),
        "cache_na_v": nrm((DEC_BATCH, N_NA, PAST_LEN, D_HEADS, D_HEAD_DIM)),
        "c": nrm((DEC_BATCH, d)),
        "c_ctx": nrm((d,)),
        "ada_w": nrm((DEPTH, d, 6 * d), 0.5 * d ** -0.5),
        "ada_b": nrm((DEPTH, 6 * d), 0.02),
        "norm_mix_pre": gain((DEPTH, d)),
        "norm_mix_post": gain((DEPTH, d)),
        "norm_ffn_pre": gain((DEPTH, d)),
        "norm_ffn_post": gain((DEPTH, d)),
        "ffn_w_gu": nrm((DEPTH, d, 2 * D_FF), d ** -0.5),
        "ffn_w_down": nrm((DEPTH, D_FF, d), D_FF ** -0.5),
        "gqa_w_qkv": nrm((N_GQA, d, (A_HEADS + 2 * A_KV_HEADS) * A_HEAD_DIM), d ** -0.5),
        "gqa_q_norm": gain((N_GQA, A_HEAD_DIM)),
        "gqa_k_norm": gain((N_GQA, A_HEAD_DIM)),
        "gqa_w_o": nrm((N_GQA, A_HEADS * A_HEAD_DIM, d), (A_HEADS * A_HEAD_DIM) ** -0.5),
        "mla_w_in": nrm((N_MLA, d, B_Q_RANK + B_KV_RANK + B_ROPE_DIM), d ** -0.5),
        "mla_q_norm": gain((N_MLA, B_Q_RANK)),
        "mla_kv_norm": gain((N_MLA, B_KV_RANK)),
        "mla_w_q_up": nrm((N_MLA, B_Q_RANK, B_HEADS * (B_NOPE_DIM + B_ROPE_DIM)), B_Q_RANK ** -0.5),
        "mla_w_kv_up": nrm((N_MLA, B_KV_RANK, B_HEADS * (B_NOPE_DIM + B_V_DIM)), B_KV_RANK ** -0.5),
        "mla_w_o": nrm((N_MLA, B_HEADS * B_V_DIM, d), (B_HEADS * B_V_DIM) ** -0.5),
        "diff_w_qkv": nrm((N_DIFF, d, 3 * C_HEADS * 2 * C_HEAD_DIM), d ** -0.5),
        "diff_lam_q1": nrm((N_DIFF, C_HEAD_DIM), 0.1),
        "diff_lam_k1": nrm((N_DIFF, C_HEAD_DIM), 0.1),
        "diff_lam_q2": nrm((N_DIFF, C_HEAD_DIM), 0.1),
        "diff_lam_k2": nrm((N_DIFF, C_HEAD_DIM), 0.1),
        "diff_subln": gain((N_DIFF, 2 * C_HEAD_DIM)),
        "diff_w_o": nrm((N_DIFF, 2 * C_HEADS * C_HEAD_DIM, d), (2 * C_HEADS * C_HEAD_DIM) ** -0.5),
        "na_w_qkv": nrm((N_NA, d, 3 * D_HEADS * D_HEAD_DIM), d ** -0.5),
        "na_rpb": nrm((N_NA, D_HEADS, 2 * WIN_R - 1, 2 * WIN_C - 1), 0.5),
        "na_w_o": nrm((N_NA, D_HEADS * D_HEAD_DIM, d), (D_HEADS * D_HEAD_DIM) ** -0.5),
    }


def reference(x_prompt, x_sample, cache_gqa_k, cache_gqa_v, cache_mla_ckv, cache_mla_kpe,
              cache_diff_k, cache_diff_v, cache_na_k, cache_na_v, c, c_ctx,
              ada_w, ada_b, norm_mix_pre, norm_mix_post, norm_ffn_pre, norm_ffn_post,
              ffn_w_gu, ffn_w_down,
              gqa_w_qkv, gqa_q_norm, gqa_k_norm, gqa_w_o,
              mla_w_in, mla_q_norm, mla_kv_norm, mla_w_q_up, mla_w_kv_up, mla_w_o,
              diff_w_qkv, diff_lam_q1, diff_lam_k1, diff_lam_q2, diff_lam_k2, diff_subln, diff_w_o,
              na_w_qkv, na_rpb, na_w_o):
    xp, xs = x_prompt, x_sample
    gqa_k_l, gqa_v_l, mla_ckv_l, mla_kpe_l, diff_k_l, diff_v_l, na_k_l, na_v_l = ([] for _ in range(8))
    for layer in range(DEPTH):
        kind, j = layer % N_MIXERS, layer // N_MIXERS
        mod_p = _adaln(c_ctx[None, :], ada_w[layer], ada_b[layer])
        mod_s = _adaln(c, ada_w[layer], ada_b[layer])
        hp = _modulated_norm(xp, norm_mix_pre[layer], mod_p[0], mod_p[1])
        hs = _modulated_norm(xs, norm_mix_pre[layer], mod_s[0], mod_s[1])
        if kind == 0:
            op, sk, sv = _gqa_context(hp, gqa_w_qkv[j], gqa_q_norm[j], gqa_k_norm[j], gqa_w_o[j])
            os_ = _gqa_latent(hs, cache_gqa_k[:, j], cache_gqa_v[:, j], gqa_w_qkv[j], gqa_q_norm[j],
                              gqa_k_norm[j], gqa_w_o[j])
            gqa_k_l.append(sk)
            gqa_v_l.append(sv)
        elif kind == 1:
            op, sk, sv = _mla_context(hp, mla_w_in[j], mla_q_norm[j], mla_kv_norm[j], mla_w_q_up[j],
                                      mla_w_kv_up[j], mla_w_o[j])
            os_ = _mla_latent(hs, cache_mla_ckv[:, j], cache_mla_kpe[:, j], mla_w_in[j], mla_q_norm[j],
                              mla_kv_norm[j], mla_w_q_up[j], mla_w_kv_up[j], mla_w_o[j])
            mla_ckv_l.append(sk)
            mla_kpe_l.append(sv)
        elif kind == 2:
            lambda_init = 0.8 - 0.6 * math.exp(-0.3 * layer)
            lam_args = (diff_lam_q1[j], diff_lam_k1[j], diff_lam_q2[j], diff_lam_k2[j], diff_subln[j],
                        diff_w_o[j], lambda_init)
            qp, sk, sv = _diff_project(hp, diff_w_qkv[j])
            op = _diff_attention(qp, sk, sv, *lam_args)
            qs, ks, vs = _diff_project(hs, diff_w_qkv[j])
            cos, sin = _axial_rope_angles(hs.shape[1], C_HEAD_DIM)
            qs, ks = _apply_rope(qs, cos, sin), _apply_rope(ks, cos, sin)
            os_ = _diff_attention(qs, jnp.concatenate([cache_diff_k[:, j], ks], axis=1),
                                  jnp.concatenate([cache_diff_v[:, j], vs], axis=1), *lam_args)
            diff_k_l.append(sk)
            diff_v_l.append(sv)
        else:
            op, sk, sv = _na_context(hp, na_w_qkv[j], na_w_o[j])
            os_ = _na_latent(hs, cache_na_k[:, j], cache_na_v[:, j], na_w_qkv[j], na_rpb[j], na_w_o[j])
            na_k_l.append(sk)
            na_v_l.append(sv)
        xp = _gated_post(xp, op, norm_mix_post[layer], mod_p[2])
        xs = _gated_post(xs, os_, norm_mix_post[layer], mod_s[2])
        hp = _modulated_norm(xp, norm_ffn_pre[layer], mod_p[3], mod_p[4])
        hs = _modulated_norm(xs, norm_ffn_pre[layer], mod_s[3], mod_s[4])
        xp = _gated_post(xp, _swiglu(hp, ffn_w_gu[layer], ffn_w_down[layer]), norm_ffn_post[layer], mod_p[5])
        xs = _gated_post(xs, _swiglu(hs, ffn_w_gu[layer], ffn_w_down[layer]), norm_ffn_post[layer], mod_s[5])
    new_gqa_k = jnp.stack(gqa_k_l, axis=1)
    new_gqa_v = jnp.stack(gqa_v_l, axis=1)
    new_mla_ckv = jnp.stack(mla_ckv_l, axis=1)
    new_mla_kpe = jnp.stack(mla_kpe_l, axis=1)
    new_diff_k = jnp.stack(diff_k_l, axis=1)
    new_diff_v = jnp.stack(diff_v_l, axis=1)
    new_na_k = jnp.stack(na_k_l, axis=1)
    new_na_v = jnp.stack(na_v_l, axis=1)
    return (xp, xs, new_gqa_k, new_gqa_v, new_mla_ckv, new_mla_kpe, new_diff_k, new_diff_v, new_na_k, new_na_v)
```

```python
import functools
import math

import numpy as np
import jax
import jax.numpy as jnp
from jax import lax
from jax.experimental import pallas as pl
from jax.experimental.pallas import tpu as pltpu

GRID_W = 64
N_MIXERS = 4
ROPE_BASE = 10000.0
NORM_EPS = 1e-6
NEG_INF = -1e30
HEAD_DIM = 128
B_ROPE_DIM = 64
WIN_R = 8
WIN_C = 16

LANES = 128
MOD_ROWS = 16
V7X_VMEM_BYTES = 64 * 1024 * 1024
VMEM_CAP_BYTES = V7X_VMEM_BYTES - 8 * 1024 * 1024

BF16 = jnp.bfloat16
F32 = jnp.float32


def _pick(n, target, mult):
    best = None
    for t in range(mult, min(n, target) + 1, mult):
        if n % t == 0:
            best = t
    return n if best is None else best


def _params(vmem_estimate_bytes, n_grid):
    limit = int(min(VMEM_CAP_BYTES, max(32 * 1024 * 1024, vmem_estimate_bytes)))
    return pltpu.CompilerParams(dimension_semantics=("arbitrary",) * n_grid, vmem_limit_bytes=limit)


def _adaln_kernel(c_ref, w_ref, b_ref, o_ref):
    c = c_ref[...]
    a = (c * jax.nn.sigmoid(c)).astype(BF16)
    o_ref[...] = jnp.dot(a, w_ref[...].astype(BF16), preferred_element_type=F32) + b_ref[...]


def _adaln(cond, ada_w, ada_b):
    depth, d, n = ada_w.shape
    tn = _pick(n, 512, LANES)
    est = 2 * d * tn * 4 + d * tn * 2 + 4 * MOD_ROWS * (d + tn) * 4
    return pl.pallas_call(
        _adaln_kernel,
        grid=(depth, n // tn),
        in_specs=[
            pl.BlockSpec((MOD_ROWS, d), lambda l, j: (0, 0)),
            pl.BlockSpec((None, d, tn), lambda l, j: (l, 0, j)),
            pl.BlockSpec((None, 1, tn), lambda l, j: (l, 0, j)),
        ],
        out_specs=pl.BlockSpec((None, MOD_ROWS, tn), lambda l, j: (l, 0, j)),
        out_shape=jax.ShapeDtypeStruct((depth, MOD_ROWS, n), F32),
        compiler_params=_params(est + (8 << 20), 2),
        name="adaln",
    )(cond, ada_w, ada_b.reshape(depth, 1, n))


def _rms(x, g):
    return x * lax.rsqrt(jnp.mean(x * x, axis=-1, keepdims=True) + NORM_EPS) * g


def _rows_kernel(*refs, has_post, has_pre):
    refs = list(refs)
    x = refs.pop(0)[...]
    if has_post:
        y_ref, gpost_ref, gate_ref = refs[:3]
        refs = refs[3:]
        x = x + gate_ref[0] * _rms(y_ref[...], gpost_ref[...])
    if has_pre:
        gpre_ref, shift_ref, scale_ref = refs[:3]
        refs = refs[3:]
    if has_post:
        refs.pop(0)[...] = x
    if has_pre:
        refs.pop(0)[...] = (_rms(x, gpre_ref[...]) * (1.0 + scale_ref[0]) + shift_ref[0]).astype(BF16)


def _rows(x, y, mods, tr, post=None, pre=None):
    t, d = x.shape
    row_spec = pl.BlockSpec((tr, d), lambda i: (i, 0))
    vec_spec = pl.BlockSpec((1, d), lambda i: (0, 0))

    def mod_spec(seg_of_tile, slot):
        return pl.BlockSpec((1, 1, d), lambda i: (seg_of_tile(i), 0, slot))

    args, in_specs, out_shape, out_specs = [x], [row_spec], [], []
    if post is not None:
        args += [y, post[0], mods]
        in_specs += [row_spec, vec_spec, mod_spec(post[1], post[2])]
        out_shape.append(jax.ShapeDtypeStruct((t, d), F32))
        out_specs.append(row_spec)
    if pre is not None:
        args += [pre[0], mods, mods]
        in_specs += [vec_spec, mod_spec(pre[1], pre[2]), mod_spec(pre[1], pre[3])]
        out_shape.append(jax.ShapeDtypeStruct((t, d), BF16))
        out_specs.append(row_spec)
    est = 2 * tr * d * 4 * (len(args) + 2) + (8 << 20)
    out = pl.pallas_call(
        functools.partial(_rows_kernel, has_post=post is not None, has_pre=pre is not None),
        grid=(t // tr,),
        in_specs=in_specs,
        out_specs=out_specs,
        out_shape=out_shape,
        compiler_params=_params(est, 1),
        name="rows",
    )(*args)
    return out


def _rope_lanes(y, tabs, half):
    if half == LANES // 2:
        return y * tabs[0] + pltpu.roll(y, LANES // 2, 1) * tabs[1]
    return (y * tabs[0] + pltpu.roll(y, LANES - half, 1) * tabs[1] + pltpu.roll(y, half, 1) * tabs[2])


def _mm_kernel(*refs, nk, swiglu, norm_width, rope_half, out_scale):
    refs = list(refs)
    a_ref = refs.pop(0)
    w_ref = refs.pop(0)
    w2_ref = refs.pop(0) if swiglu else None
    gain_ref = refs.pop(0) if norm_width else None
    n_tab = 0 if not rope_half else (2 if rope_half == LANES // 2 else 3)
    tab_refs = [refs.pop(0) for _ in range(n_tab)]
    o_ref = refs.pop(0)
    acc_refs = refs

    a = a_ref[...].astype(BF16)
    parts = [jnp.dot(a, w_ref[...], preferred_element_type=F32)]
    if swiglu:
        parts.append(jnp.dot(a, w2_ref[...], preferred_element_type=F32))

    def epilogue(vals):
        y = vals[0]
        if swiglu:
            y = (y * jax.nn.sigmoid(y)) * vals[1]
        bn = y.shape[-1]
        if norm_width or rope_half:
            tabs = [r[...] for r in tab_refs]
            width = norm_width if norm_width else LANES
            for c in range(bn // width):
                sl = slice(c * width, (c + 1) * width)
                yc = y[:, sl]
                if norm_width:
                    yc = _rms(yc, gain_ref[:, sl])
                if rope_half:
                    yc = _rope_lanes(yc, tabs, rope_half)
                if out_scale != 1.0:
                    yc = yc * out_scale
                o_ref[:, sl] = yc.astype(o_ref.dtype)
        else:
            if out_scale != 1.0:
                y = y * out_scale
            o_ref[...] = y.astype(o_ref.dtype)

    if nk == 1:
        epilogue(parts)
    else:
        k = pl.program_id(2)

        @pl.when(k == 0)
        def _():
            for r, p in zip(acc_refs, parts):
                r[...] = p

        @pl.when(k > 0)
        def _():
            for r, p in zip(acc_refs, parts):
                r[...] += p

        @pl.when(k == nk - 1)
        def _():
            epilogue([r[...] for r in acc_refs])


def _matmul(a, w, out_dtype, *, w_cols=None, swiglu=False, gain=None, norm_width=0, rope_tabs=None,
            rope_half=0, out_scale=1.0, bm=1024, bn=512, bk=None):
    m, kdim = a.shape
    col0, n = (0, w.shape[1]) if w_cols is None else w_cols
    if swiglu:
        col0, n = 0, w.shape[1] // 2
    bm = _pick(m, bm, 8)
    bn = _pick(math.gcd(n, col0) if col0 else n, bn, LANES)
    if norm_width > LANES:
        bn = norm_width
    bk = kdim if bk is None else _pick(kdim, bk, LANES)
    nk = kdim // bk
    jb = col0 // bn

    args = [a, w]
    in_specs = [pl.BlockSpec((bm, bk), lambda i, j, k: (i, k)),
                pl.BlockSpec((bk, bn), lambda i, j, k: (k, j + jb))]
    if swiglu:
        ju = n // bn
        args.append(w)
        in_specs.append(pl.BlockSpec((bk, bn), lambda i, j, k: (k, j + ju)))
    if norm_width:
        args.append(gain)
        in_specs.append(pl.BlockSpec((1, bn), lambda i, j, k: (0, j)))
    if rope_half:
        for tab in rope_tabs:
            args.append(tab)
            in_specs.append(pl.BlockSpec((bm, LANES), lambda i, j, k: (i, 0)))
    n_w = 2 if swiglu else 1
    scratch = [pltpu.VMEM((bm, bn), F32) for _ in range(n_w)] if nk > 1 else []
    est = (2 * bm * bk * a.dtype.itemsize + 2 * n_w * bk * bn * 2 + 2 * bm * bn * jnp.dtype(out_dtype).itemsize
           + (n_w + 3) * bm * bn * 4 + (bm * bk * 2 if a.dtype != BF16 else 0) + (4 << 20))
    return pl.pallas_call(
        functools.partial(_mm_kernel, nk=nk, swiglu=swiglu, norm_width=norm_width, rope_half=rope_half,
                          out_scale=out_scale),
        grid=(m // bm, n // bn, nk),
        in_specs=in_specs,
        out_specs=pl.BlockSpec((bm, bn), lambda i, j, k: (i, j)),
        out_shape=jax.ShapeDtypeStruct((m, n), out_dtype),
        scratch_shapes=scratch,
        compiler_params=_params(est, 3),
        name="matmul",
    )(*args)


def _dot_nt(q, k):
    return lax.dot_general(q, k, (((1,), (1,)), ((), ())), preferred_element_type=F32)


def _softmax_parts(s_parts):
    m = s_parts[0].max(axis=-1, keepdims=True)
    for s in s_parts[1:]:
        m = jnp.maximum(m, s.max(axis=-1, keepdims=True))
    e_parts = [jnp.exp(s - m) for s in s_parts]
    denom = e_parts[0].sum(axis=-1, keepdims=True)
    for e in e_parts[1:]:
        denom = denom + e.sum(axis=-1, keepdims=True)
    return e_parts, denom


def _band_start(tq, n_rows, band_rows):
    r0 = pl.program_id(2) * (tq // GRID_W)
    wr = min(WIN_R, n_rows)
    return pl.multiple_of(jnp.clip(r0 - wr // 2, 0, n_rows - band_rows) * GRID_W, GRID_W)


def _attn_plain_kernel(*refs, n_parts, band):
    refs = list(refs)
    q_ref = refs.pop(0)
    kv_refs = [(refs.pop(0), refs.pop(0)) for _ in range(n_parts)]
    bias_ref = refs.pop(0) if band else None
    o_ref = refs.pop(0)
    q = q_ref[...]
    s_parts, vs = [], []
    for p, (k_ref, v_ref) in enumerate(kv_refs):
        if band and p == n_parts - 1:
            rows = pl.ds(_band_start(q_ref.shape[0], band[0], band[1]), band[1] * GRID_W)
            s_parts.append(_dot_nt(q, k_ref[rows, :].astype(BF16)) + bias_ref[...])
            vs.append(v_ref[rows, :].astype(BF16))
        else:
            s_parts.append(_dot_nt(q, k_ref[...].astype(BF16)))
            vs.append(v_ref[...].astype(BF16))
    e_parts, denom = _softmax_parts(s_parts)
    o = jnp.dot(e_parts[0].astype(BF16), vs[0], preferred_element_type=F32)
    for e, v in zip(e_parts[1:], vs[1:]):
        o = o + jnp.dot(e.astype(BF16), v, preferred_element_type=F32)
    o_ref[...] = (o / denom).astype(o_ref.dtype)


def _attn_mla_kernel(*refs, n_parts):
    refs = list(refs)
    qn = refs.pop(0)[...]
    qp = refs.pop(0)[...]
    s_parts, vs = [], []
    for _ in range(n_parts):
        kn_ref, v_ref, kp_ref = refs.pop(0), refs.pop(0), refs.pop(0)
        s_parts.append(_dot_nt(qn, kn_ref[...]) + _dot_nt(qp, kp_ref[...].astype(BF16)))
        vs.append(v_ref[...])
    o_ref = refs.pop(0)
    e_parts, denom = _softmax_parts(s_parts)
    o = jnp.dot(e_parts[0].astype(BF16), vs[0], preferred_element_type=F32)
    for e, v in zip(e_parts[1:], vs[1:]):
        o = o + jnp.dot(e.astype(BF16), v, preferred_element_type=F32)
    o_ref[...] = (o / denom).astype(o_ref.dtype)


def _attn_diff_kernel(*refs, n_parts, lambda_init):
    refs = list(refs)
    lam_ref = refs.pop(0)
    g_ref = refs.pop(0)
    q_ref = refs.pop(0)
    kv_refs = [(refs.pop(0), refs.pop(0)) for _ in range(n_parts)]
    o_ref = refs.pop(0)
    d = HEAD_DIM
    lam = (jnp.exp(jnp.sum(lam_ref[0:1, :] * lam_ref[1:2, :], axis=-1, keepdims=True))
           - jnp.exp(jnp.sum(lam_ref[2:3, :] * lam_ref[3:4, :], axis=-1, keepdims=True)) + lambda_init)
    maps = []
    for j in range(2):
        q = q_ref[:, j * d:(j + 1) * d]
        s_parts = [_dot_nt(q, k_ref[:, j * d:(j + 1) * d].astype(BF16)) for k_ref, _ in kv_refs]
        maps.append(_softmax_parts(s_parts))
    (e0, l0), (e1, l1) = maps
    c0 = 1.0 / l0
    c1 = lam / l1
    o = None
    for p, (_, v_ref) in enumerate(kv_refs):
        wgt = (e0[p] * c0 - e1[p] * c1).astype(BF16)
        contrib = jnp.dot(wgt, v_ref[...].astype(BF16), preferred_element_type=F32)
        o = contrib if o is None else o + contrib
    o_ref[...] = (_rms(o, g_ref[...]) * (1.0 - lambda_init)).astype(o_ref.dtype)


def _drop_ref(kernel_fn, index):
    def wrapped(*refs):
        return kernel_fn(*refs[:index], *refs[index + 1:])
    return wrapped


def _attention(kernel_fn, grid, operands, out_cols, out_block, out_map, t_rows, est, name, into=None):
    in_specs = [pl.BlockSpec(blk, imap) for _, blk, imap in operands]
    args = [arr for arr, _, _ in operands]
    aliases = {}
    if into is not None:
        kernel_fn = _drop_ref(kernel_fn, len(args))
        aliases = {len(args): 0}
        in_specs.append(pl.BlockSpec(memory_space=pl.ANY))
        args.append(into)
    return pl.pallas_call(
        kernel_fn,
        grid=grid,
        in_specs=in_specs,
        out_specs=pl.BlockSpec(out_block, out_map),
        out_shape=jax.ShapeDtypeStruct((t_rows, out_cols), BF16),
        input_output_aliases=aliases,
        compiler_params=_params(est, len(grid)),
        name=name,
    )(*args)


def _rope_tables(n_ctx_rows, dec_batch, dec_seq, rot_dim):
    t = jnp.arange(dec_seq)
    per_axis = rot_dim // 2
    inv_freq = ROPE_BASE ** (-jnp.arange(0, per_axis, 2, dtype=F32) / per_axis)
    row = (t // GRID_W).astype(F32)[:, None] * inv_freq
    col = (t % GRID_W).astype(F32)[:, None] * inv_freq
    ang = jnp.concatenate([row, col], axis=-1)
    cos, sin = jnp.cos(ang), jnp.sin(ang)
    pad = LANES - rot_dim
    ones = jnp.ones((dec_seq, pad), F32)
    zeros_half = jnp.zeros_like(sin)
    zeros_pad = jnp.zeros((dec_seq, pad), F32)
    if pad == 0:
        tabs = [jnp.concatenate([cos, cos], -1), jnp.concatenate([-sin, sin], -1)]
    else:
        tabs = [jnp.concatenate([cos, cos, ones], -1),
                jnp.concatenate([-sin, zeros_half, zeros_pad], -1),
                jnp.concatenate([zeros_half, sin, zeros_pad], -1)]
    out = []
    for i, tab in enumerate(tabs):
        ctx = jnp.ones((n_ctx_rows, LANES), F32) if i == 0 else jnp.zeros((n_ctx_rows, LANES), F32)
        out.append(jnp.concatenate([ctx, jnp.tile(tab, (dec_batch, 1))], axis=0))
    return out


def _na_band_plan(n_rows, tq):
    wr = min(WIN_R, n_rows)
    rt = tq // GRID_W
    rows_per_lane_tile = LANES // GRID_W
    band_rows = min(n_rows, -(-(wr + rt - 1) // rows_per_lane_tile) * rows_per_lane_tile)
    cases, case_of_tile = [], []
    for qt in range(n_rows // rt):
        r0 = qt * rt
        bs = int(np.clip(r0 - wr // 2, 0, n_rows - band_rows))
        r = r0 + np.arange(rt)[:, None]
        kr = bs + np.arange(band_rows)[None, :]
        rs = np.clip(r - wr // 2, 0, n_rows - wr)
        valid = (kr >= rs) & (kr < rs + wr)
        ridx = np.clip(kr - r + WIN_R - 1, 0, 2 * WIN_R - 2)
        key = (valid.tobytes(), ridx.tobytes())
        for ci, (k0, _, _) in enumerate(cases):
            if k0 == key:
                break
        else:
            ci = len(cases)
            cases.append((key, ridx, valid))
        case_of_tile.append(ci)
    if any(b < a for a, b in zip(case_of_tile, case_of_tile[1:])) or \
            sorted(set(case_of_tile)) != list(range(len(cases))):
        raise NotImplementedError("neighbourhood tile geometries are not monotone in the tile index")
    thresholds = [case_of_tile.index(ci) for ci in range(1, len(cases))]
    ridx = np.stack([c[1] for c in cases])
    valid = np.stack([c[2] for c in cases])
    return band_rows, ridx, valid, thresholds


def _na_bias(rpb, ridx, valid):
    col = jnp.arange(GRID_W)
    col_start = jnp.clip(col - WIN_C // 2, 0, GRID_W - WIN_C)
    in_win = (col[None, :] >= col_start[:, None]) & (col[None, :] < col_start[:, None] + WIN_C)
    col_off = jnp.clip(col[None, :] - col[:, None] + WIN_C - 1, 0, 2 * WIN_C - 2)
    col_bias = jnp.where(in_win[None, None], rpb[:, :, col_off].astype(F32), NEG_INF)
    n_case, rt, band_rows = ridx.shape
    blk = jnp.take(col_bias, jnp.asarray(ridx.reshape(-1)), axis=1)
    blk = blk.reshape(rpb.shape[0], n_case, rt, band_rows, GRID_W, GRID_W)
    blk = jnp.where(jnp.asarray(valid)[None, :, :, :, None, None], blk, NEG_INF)
    blk = jnp.transpose(blk, (0, 1, 2, 4, 3, 5))
    return blk.reshape(rpb.shape[0], n_case, rt * GRID_W, band_rows * GRID_W)


def kernel(x_prompt, x_sample, cache_gqa_k, cache_gqa_v, cache_mla_ckv, cache_mla_kpe, cache_diff_k, cache_diff_v, cache_na_k, cache_na_v, c, c_ctx, ada_w, ada_b, norm_mix_pre, norm_mix_post, norm_ffn_pre, norm_ffn_post, ffn_w_gu, ffn_w_down, gqa_w_qkv, gqa_q_norm, gqa_k_norm, gqa_w_o, mla_w_in, mla_q_norm, mla_kv_norm, mla_w_q_up, mla_w_kv_up, mla_w_o, diff_w_qkv, diff_lam_q1, diff_lam_k1, diff_lam_q2, diff_lam_k2, diff_subln, diff_w_o, na_w_qkv, na_rpb, na_w_o):
    nb, seq, d = x_prompt.shape
    db, dseq, _ = x_sample.shape
    past = cache_gqa_k.shape[2]
    depth = ada_w.shape[0]
    tp, ts = nb * seq, db * dseq
    t = tp + ts
    hd = HEAD_DIM
    assert db + 1 <= MOD_ROWS and dseq % GRID_W == 0
    assert tp % dseq == 0 and t % past == 0 and tp % past == 0

    tr = _pick(math.gcd(seq, dseq), 256, 8)
    n_ctx_tiles, tiles_per_dec = tp // tr, dseq // tr

    def seg_of_tile_at(layer):
        return lambda i: layer * MOD_ROWS + jnp.where(i < n_ctx_tiles, 0, 1 + (i - n_ctx_tiles) // tiles_per_dec)

    cond = jnp.concatenate([c_ctx[None, :], c, jnp.zeros((MOD_ROWS - 1 - db, d), F32)], axis=0)
    mods = _adaln(cond, ada_w, ada_b).reshape(depth * MOD_ROWS, 1, 6 * d)

    x = jnp.concatenate([x_prompt.reshape(tp, d), x_sample.reshape(ts, d)], axis=0)

    rope128 = _rope_tables(tp, db, dseq, hd)
    rope64 = _rope_tables(tp, db, dseq, B_ROPE_DIM)

    tq_ctx = seq
    tq = _pick(dseq, 512, GRID_W)
    ctx_tiles_q = tp // tq
    lat_blk0 = tp // dseq
    n_qt = dseq // tq

    def vec(v):
        return v.reshape(1, -1).astype(F32)

    def ctx_out(arr, shape):
        return arr[:tp].reshape((nb, seq) + shape)

    outs = {k: [] for k in ("gqa_k", "gqa_v", "mla_ckv", "mla_kpe", "diff_k", "diff_v", "na_k", "na_v")}

    def plain_attention(q, k, v, ck, cv, heads, group, tq_lat, bias_operand, band, name):
        n_qt_lat, ctx_tiles = dseq // tq_lat, tp // tq_lat
        est = 6 * tq_lat * (past + dseq) * 4 + 6 * (past + dseq) * hd * 4 + (8 << 20)
        o = _attention(
            functools.partial(_attn_plain_kernel, n_parts=1, band=None),
            (nb, heads),
            [(q, (tq_ctx, hd), lambda b, g: (b, g)),
             (k, (seq, hd), lambda b, g: (b, g // group)),
             (v, (seq, hd), lambda b, g: (b, g // group))],
            heads * hd, (tq_ctx, hd), lambda b, g: (b, g), t, est, name + "_ctx")
        operands = [(q, (tq_lat, hd), lambda b, g, i: (ctx_tiles + b * n_qt_lat + i, g)),
                    (ck, (past, hd), lambda b, g, i: (b, g // group)),
                    (cv, (past, hd), lambda b, g, i: (b, g // group)),
                    (k, (dseq, hd), lambda b, g, i: (lat_blk0 + b, g // group)),
                    (v, (dseq, hd), lambda b, g, i: (lat_blk0 + b, g // group))]
        if bias_operand is not None:
            operands.append(bias_operand)
        return _attention(
            functools.partial(_attn_plain_kernel, n_parts=2, band=band),
            (db, heads, n_qt_lat), operands,
            heads * hd, (tq_lat, hd), lambda b, g, i: (ctx_tiles + b * n_qt_lat + i, g), t, est, name + "_lat",
            into=o)

    (h,) = _rows(x, None, mods, tr, pre=(vec(norm_mix_pre[0]), seg_of_tile_at(0), 0, 1))

    for layer in range(depth):
        kind, j = layer % N_MIXERS, layer // N_MIXERS
        seg = seg_of_tile_at(layer)

        if kind == 0:
            w = gqa_w_qkv[j].astype(BF16)
            n_q = gqa_w_o.shape[1]
            n_kv = (w.shape[1] - n_q) // 2
            heads, kv_heads = n_q // hd, n_kv // hd
            group = heads // kv_heads
            scale = hd ** -0.5
            q = _matmul(h, w, BF16, w_cols=(0, n_q), gain=jnp.tile(vec(gqa_q_norm[j]), (1, heads)), norm_width=hd,
                        rope_tabs=rope128, rope_half=hd // 2, out_scale=scale)
            k = _matmul(h, w, F32, w_cols=(n_q, n_kv), gain=jnp.tile(vec(gqa_k_norm[j]), (1, kv_heads)),
                        norm_width=hd, rope_tabs=rope128, rope_half=hd // 2)
            v = _matmul(h, w, F32, w_cols=(n_q + n_kv, n_kv))
            outs["gqa_k"].append(ctx_out(k, (kv_heads, hd)))
            outs["gqa_v"].append(ctx_out(v, (kv_heads, hd)))
            ck = cache_gqa_k[:, j].reshape(db * past, n_kv)
            cv = cache_gqa_v[:, j].reshape(db * past, n_kv)
            o = plain_attention(q, k, v, ck, cv, heads, group, tq, None, None, "attn_gqa")
            w_o = gqa_w_o[j]

        elif kind == 1:
            q_rank, kv_rank = mla_q_norm.shape[1], mla_kv_norm.shape[1]
            w_in = mla_w_in[j].astype(BF16)
            rope_dim = w_in.shape[1] - q_rank - kv_rank
            w_kpe = jnp.pad(w_in[:, q_rank + kv_rank:], ((0, 0), (0, LANES - rope_dim)))
            heads = mla_w_o.shape[1] // hd
            scale = (hd + rope_dim) ** -0.5
            cq = _matmul(h, w_in, BF16, w_cols=(0, q_rank), gain=vec(mla_q_norm[j]), norm_width=q_rank)
            ckv = _matmul(h, w_in, F32, w_cols=(q_rank, kv_rank), gain=vec(mla_kv_norm[j]), norm_width=kv_rank)
            kpe = _matmul(h, w_kpe, F32, rope_tabs=rope64, rope_half=rope_dim // 2)
            outs["mla_ckv"].append(ctx_out(ckv, (kv_rank,)))
            outs["mla_kpe"].append(ctx_out(kpe[:, :rope_dim], (rope_dim,)))
            w_q_up = mla_w_q_up[j].astype(BF16).reshape(q_rank, heads, hd + rope_dim)
            w_qn = w_q_up[:, :, :hd].reshape(q_rank, heads * hd)
            w_qp = jnp.pad(w_q_up[:, :, hd:], ((0, 0), (0, 0), (0, LANES - rope_dim))).reshape(q_rank, heads * LANES)
            qn = _matmul(cq, w_qn, BF16, out_scale=scale)
            qp = _matmul(cq, w_qp, BF16, rope_tabs=rope64, rope_half=rope_dim // 2, out_scale=scale)
            ckv_all = jnp.concatenate([ckv, cache_mla_ckv[:, j].reshape(db * past, kv_rank)], axis=0)
            kv = _matmul(ckv_all, mla_w_kv_up[j].astype(BF16), BF16)
            ckpe = jnp.pad(cache_mla_kpe[:, j].reshape(db * past, rope_dim), ((0, 0), (0, LANES - rope_dim)))
            cache_blk0 = t // past
            est = 8 * tq * (past + dseq) * 4 + 8 * (past + dseq) * hd * 4 + (8 << 20)
            o_ctx = _attention(
                functools.partial(_attn_mla_kernel, n_parts=1),
                (nb, heads),
                [(qn, (tq_ctx, hd), lambda b, g: (b, g)),
                 (qp, (tq_ctx, LANES), lambda b, g: (b, g)),
                 (kv, (seq, hd), lambda b, g: (b, 2 * g)),
                 (kv, (seq, hd), lambda b, g: (b, 2 * g + 1)),
                 (kpe, (seq, LANES), lambda b, g: (b, 0))],
                heads * hd, (tq_ctx, hd), lambda b, g: (b, g), t, est, "attn_mla_ctx")
            o = _attention(
                functools.partial(_attn_mla_kernel, n_parts=2),
                (db, heads, n_qt),
                [(qn, (tq, hd), lambda b, g, i: (ctx_tiles_q + b * n_qt + i, g)),
                 (qp, (tq, LANES), lambda b, g, i: (ctx_tiles_q + b * n_qt + i, g)),
                 (kv, (past, hd), lambda b, g, i: (cache_blk0 + b, 2 * g)),
                 (kv, (past, hd), lambda b, g, i: (cache_blk0 + b, 2 * g + 1)),
                 (ckpe, (past, LANES), lambda b, g, i: (b, 0)),
                 (kv, (dseq, hd), lambda b, g, i: (lat_blk0 + b, 2 * g)),
                 (kv, (dseq, hd), lambda b, g, i: (lat_blk0 + b, 2 * g + 1)),
                 (kpe, (dseq, LANES), lambda b, g, i: (lat_blk0 + b, 0))],
                heads * hd, (tq, hd), lambda b, g, i: (ctx_tiles_q + b * n_qt + i, g), t, est, "attn_mla_lat",
                into=o_ctx)
            w_o = mla_w_o[j]

        elif kind == 2:
            lambda_init = 0.8 - 0.6 * math.exp(-0.3 * layer)
            w = diff_w_qkv[j].astype(BF16)
            n_w = w.shape[1] // 3
            heads = n_w // (2 * hd)
            scale = hd ** -0.5
            q = _matmul(h, w, BF16, w_cols=(0, n_w), rope_tabs=rope128, rope_half=hd // 2, out_scale=scale)
            k = _matmul(h, w, F32, w_cols=(n_w, n_w), rope_tabs=rope128, rope_half=hd // 2)
            v = _matmul(h, w, F32, w_cols=(2 * n_w, n_w))
            outs["diff_k"].append(ctx_out(k, (heads, 2, hd)))
            outs["diff_v"].append(ctx_out(v, (heads, 2 * hd)))
            ck = cache_diff_k[:, j].reshape(db * past, n_w)
            cv = cache_diff_v[:, j].reshape(db * past, n_w)
            lam = jnp.stack([diff_lam_q1[j], diff_lam_k1[j], diff_lam_q2[j], diff_lam_k2[j]]).astype(F32)
            subln = vec(diff_subln[j])
            hw = 2 * hd
            est = 10 * tq * (past + dseq) * 4 + 8 * (past + dseq) * hw * 4 + (8 << 20)
            o_ctx = _attention(
                functools.partial(_attn_diff_kernel, n_parts=1, lambda_init=lambda_init),
                (nb, heads),
                [(lam, (4, hd), lambda b, g: (0, 0)),
                 (subln, (1, hw), lambda b, g: (0, 0)),
                 (q, (tq_ctx, hw), lambda b, g: (b, g)),
                 (k, (seq, hw), lambda b, g: (b, g)),
                 (v, (seq, hw), lambda b, g: (b, g))],
                n_w, (tq_ctx, hw), lambda b, g: (b, g), t, est, "attn_diff_ctx")
            o = _attention(
                functools.partial(_attn_diff_kernel, n_parts=2, lambda_init=lambda_init),
                (db, heads, n_qt),
                [(lam, (4, hd), lambda b, g, i: (0, 0)),
                 (subln, (1, hw), lambda b, g, i: (0, 0)),
                 (q, (tq, hw), lambda b, g, i: (ctx_tiles_q + b * n_qt + i, g)),
                 (ck, (past, hw), lambda b, g, i: (b, g)),
                 (cv, (past, hw), lambda b, g, i: (b, g)),
                 (k, (dseq, hw), lambda b, g, i: (lat_blk0 + b, g)),
                 (v, (dseq, hw), lambda b, g, i: (lat_blk0 + b, g))],
                n_w, (tq, hw), lambda b, g, i: (ctx_tiles_q + b * n_qt + i, g), t, est, "attn_diff_lat",
                into=o_ctx)
            w_o = diff_w_o[j]

        else:
            w = na_w_qkv[j].astype(BF16)
            n_w = w.shape[1] // 3
            heads = n_w // hd
            scale = hd ** -0.5
            q = _matmul(h, w, BF16, w_cols=(0, n_w), out_scale=scale)
            k = _matmul(h, w, F32, w_cols=(n_w, n_w))
            v = _matmul(h, w, F32, w_cols=(2 * n_w, n_w))
            outs["na_k"].append(ctx_out(k, (heads, hd)))
            outs["na_v"].append(ctx_out(v, (heads, hd)))
            ck = cache_na_k[:, j].reshape(db * past, n_w)
            cv = cache_na_v[:, j].reshape(db * past, n_w)
            n_rows = dseq // GRID_W
            tq_na = _pick(dseq, 256, GRID_W)
            band_rows, ridx, valid, thresholds = _na_band_plan(n_rows, tq_na)
            bias = _na_bias(na_rpb[j], ridx, valid)

            def case_of(i):
                return sum((i >= th).astype(jnp.int32) for th in thresholds) if thresholds else 0

            bias_operand = (bias, (None, None, tq_na, band_rows * GRID_W), lambda b, g, i: (g, case_of(i), 0, 0))
            o = plain_attention(q, k, v, ck, cv, heads, 1, tq_na, bias_operand, (n_rows, band_rows), "attn_na")
            w_o = na_w_o[j]

        y = _matmul(o, w_o.astype(BF16), F32)
        x, h = _rows(x, y, mods, tr, post=(vec(norm_mix_post[layer]), seg, 2),
                     pre=(vec(norm_ffn_pre[layer]), seg, 3, 4))

        act = _matmul(h, ffn_w_gu[layer].astype(BF16), BF16, swiglu=True, bn=256)
        y = _matmul(act, ffn_w_down[layer].astype(BF16), F32, bk=ffn_w_down.shape[1] // 2)
        post = (vec(norm_ffn_post[layer]), seg, 5)
        if layer + 1 < depth:
            x, h = _rows(x, y, mods, tr, post=post,
                         pre=(vec(norm_mix_pre[layer + 1]), seg_of_tile_at(layer + 1), 0, 1))
        else:
            (x,) = _rows(x, y, mods, tr, post=post)

    y_prompt = x[:tp].reshape(nb, seq, d)
    y_sample = x[tp:].reshape(db, dseq, d)
    return (y_prompt, y_sample,
            jnp.stack(outs["gqa_k"], axis=1), jnp.stack(outs["gqa_v"], axis=1),
            jnp.stack(outs["mla_ckv"], axis=1), jnp.stack(outs["mla_kpe"], axis=1),
            jnp.stack(outs["diff_k"], axis=1), jnp.stack(outs["diff_v"], axis=1),
            jnp.stack(outs["na_k"], axis=1), jnp.stack(outs["na_v"], axis=1))
```

```python
import functools
import math

import numpy as np
import jax
import jax.numpy as jnp
from jax import lax
from jax.experimental import pallas as pl
from jax.experimental.pallas import tpu as pltpu

GRID_W = 64
N_MIXERS = 4
ROPE_BASE = 10000.0
NORM_EPS = 1e-6
NEG_INF = -1e30
HEAD_DIM = 128
B_ROPE_DIM = 64
WIN_R = 8
WIN_C = 16
LOG2_E = math.log2(math.e)

LANES = 128
MOD_ROWS = 16
V7X_VMEM_BYTES = 64 * 1024 * 1024
VMEM_CAP_BYTES = V7X_VMEM_BYTES - 8 * 1024 * 1024

BF16 = jnp.bfloat16
F32 = jnp.float32


def _pick(n, target, mult):
    best = None
    for t in range(mult, min(n, target) + 1, mult):
        if n % t == 0:
            best = t
    return n if best is None else best


def _params(vmem_estimate_bytes, n_grid):
    limit = int(min(VMEM_CAP_BYTES, max(32 * 1024 * 1024, vmem_estimate_bytes)))
    return pltpu.CompilerParams(dimension_semantics=("arbitrary",) * n_grid, vmem_limit_bytes=limit)


def _adaln_kernel(c_ref, w_ref, b_ref, o_ref):
    c = c_ref[...]
    a = (c * jax.nn.sigmoid(c)).astype(BF16)
    o_ref[...] = jnp.dot(a, w_ref[...].astype(BF16), preferred_element_type=F32) + b_ref[...]


def _adaln(cond, ada_w, ada_b):
    depth, d, n = ada_w.shape
    tn = _pick(n, 512, LANES)
    est = 2 * d * tn * 4 + d * tn * 2 + 4 * MOD_ROWS * (d + tn) * 4
    return pl.pallas_call(
        _adaln_kernel,
        grid=(depth, n // tn),
        in_specs=[
            pl.BlockSpec((MOD_ROWS, d), lambda l, j: (0, 0)),
            pl.BlockSpec((None, d, tn), lambda l, j: (l, 0, j)),
            pl.BlockSpec((None, 1, tn), lambda l, j: (l, 0, j)),
        ],
        out_specs=pl.BlockSpec((None, MOD_ROWS, tn), lambda l, j: (l, 0, j)),
        out_shape=jax.ShapeDtypeStruct((depth, MOD_ROWS, n), F32),
        compiler_params=_params(est + (8 << 20), 2),
        name="adaln",
    )(cond, ada_w, ada_b.reshape(depth, 1, n))


def _rms(x, g):
    return x * lax.rsqrt(jnp.mean(x * x, axis=-1, keepdims=True) + NORM_EPS) * g


def _rows_kernel(*refs, has_post, has_pre):
    refs = list(refs)
    x = refs.pop(0)[...]
    if has_post:
        y_ref, gpost_ref, gate_ref = refs[:3]
        refs = refs[3:]
        x = x + gate_ref[0] * _rms(y_ref[...], gpost_ref[...])
    if has_pre:
        gpre_ref, shift_ref, scale_ref = refs[:3]
        refs = refs[3:]
    if has_post:
        refs.pop(0)[...] = x
    if has_pre:
        refs.pop(0)[...] = (_rms(x, gpre_ref[...]) * (1.0 + scale_ref[0]) + shift_ref[0]).astype(BF16)


def _rows(x, y, mods, tr, post=None, pre=None, tile0=0, n_tiles=None):
    t, d = x.shape
    n_tiles = t // tr if n_tiles is None else n_tiles
    row_spec = pl.BlockSpec((tr, d), lambda i: (i + tile0, 0))
    out_spec = pl.BlockSpec((tr, d), lambda i: (i, 0))
    vec_spec = pl.BlockSpec((1, d), lambda i: (0, 0))

    def mod_spec(seg_of_tile, slot):
        return pl.BlockSpec((1, 1, d), lambda i: (seg_of_tile(i + tile0), 0, slot))

    args, in_specs, out_shape, out_specs = [x], [row_spec], [], []
    if post is not None:
        args += [y, post[0], mods]
        in_specs += [row_spec, vec_spec, mod_spec(post[1], post[2])]
        out_shape.append(jax.ShapeDtypeStruct((n_tiles * tr, d), F32))
        out_specs.append(out_spec)
    if pre is not None:
        args += [pre[0], mods, mods]
        in_specs += [vec_spec, mod_spec(pre[1], pre[2]), mod_spec(pre[1], pre[3])]
        out_shape.append(jax.ShapeDtypeStruct((n_tiles * tr, d), BF16))
        out_specs.append(out_spec)
    est = 2 * tr * d * 4 * (len(args) + 2) + (8 << 20)
    out = pl.pallas_call(
        functools.partial(_rows_kernel, has_post=post is not None, has_pre=pre is not None),
        grid=(n_tiles,),
        in_specs=in_specs,
        out_specs=out_specs,
        out_shape=out_shape,
        compiler_params=_params(est, 1),
        name="rows",
    )(*args)
    return out


def _rope_lanes(y, tabs, half):
    if half == LANES // 2:
        return y * tabs[0] + pltpu.roll(y, LANES // 2, 1) * tabs[1]
    return (y * tabs[0] + pltpu.roll(y, LANES - half, 1) * tabs[1] + pltpu.roll(y, half, 1) * tabs[2])


def _mm_kernel(*refs, nk, swiglu, norm_width, rope_half, rope_every, out_scale):
    refs = list(refs)
    a_ref = refs.pop(0)
    w_ref = refs.pop(0)
    w2_ref = refs.pop(0) if swiglu else None
    gain_ref = refs.pop(0) if norm_width else None
    n_tab = 0 if not rope_half else (2 if rope_half == LANES // 2 else 3)
    tab_refs = [refs.pop(0) for _ in range(n_tab)]
    o_ref = refs.pop(0)
    acc_refs = refs

    a = a_ref[...].astype(BF16)
    parts = [jnp.dot(a, w_ref[...], preferred_element_type=F32)]
    if swiglu:
        parts.append(jnp.dot(a, w2_ref[...], preferred_element_type=F32))

    def epilogue(vals):
        y = vals[0]
        if swiglu:
            y = (y * jax.nn.sigmoid(y)) * vals[1]
        bn = y.shape[-1]
        if norm_width or rope_half:
            tabs = [r[...] for r in tab_refs]
            width = norm_width if norm_width else LANES
            for c in range(bn // width):
                sl = slice(c * width, (c + 1) * width)
                yc = y[:, sl]
                if norm_width:
                    yc = _rms(yc, gain_ref[:, sl])
                if rope_half and c % rope_every == rope_every - 1:
                    yc = _rope_lanes(yc, tabs, rope_half)
                if out_scale != 1.0:
                    yc = yc * out_scale
                o_ref[:, sl] = yc.astype(o_ref.dtype)
        else:
            if out_scale != 1.0:
                y = y * out_scale
            o_ref[...] = y.astype(o_ref.dtype)

    if nk == 1:
        epilogue(parts)
    else:
        k = pl.program_id(2)

        @pl.when(k == 0)
        def _():
            for r, p in zip(acc_refs, parts):
                r[...] = p

        @pl.when(k > 0)
        def _():
            for r, p in zip(acc_refs, parts):
                r[...] += p

        @pl.when(k == nk - 1)
        def _():
            epilogue([r[...] for r in acc_refs])


def _matmul(a, w, out_dtype, *, layer=0, w_cols=None, swiglu=False, gain=None, norm_width=0, rope_tabs=None,
            rope_half=0, rope_every=1, out_scale=1.0, bm=1024, bn=512, bk=None):
    m, kdim = a.shape
    col0, n = (0, w.shape[-1]) if w_cols is None else w_cols
    if swiglu:
        col0, n = 0, w.shape[-1] // 2
    bm = _pick(m, bm, 8)
    bn = _pick(math.gcd(n, col0) if col0 else n, bn, LANES * rope_every)
    if norm_width > LANES:
        bn = norm_width
    bk = kdim if bk is None else _pick(kdim, bk, LANES)
    nk = kdim // bk
    jb = col0 // bn

    def w_spec(col_blk0):
        if w.ndim == 3:
            return pl.BlockSpec((None, bk, bn), lambda i, j, k: (layer, k, j + col_blk0))
        return pl.BlockSpec((bk, bn), lambda i, j, k: (k, j + col_blk0))

    args = [a, w]
    in_specs = [pl.BlockSpec((bm, bk), lambda i, j, k: (i, k)), w_spec(jb)]
    if swiglu:
        args.append(w)
        in_specs.append(w_spec(n // bn))
    if norm_width:
        args.append(gain)
        in_specs.append(pl.BlockSpec((1, bn), lambda i, j, k: (0, j)))
    if rope_half:
        for tab in rope_tabs:
            args.append(tab)
            in_specs.append(pl.BlockSpec((bm, LANES), lambda i, j, k: (i, 0)))
    n_w = 2 if swiglu else 1
    scratch = [pltpu.VMEM((bm, bn), F32) for _ in range(n_w)] if nk > 1 else []
    est = (2 * bm * bk * a.dtype.itemsize + 2 * n_w * bk * bn * 2 + 2 * bm * bn * jnp.dtype(out_dtype).itemsize
           + (n_w + 3) * bm * bn * 4 + (bm * bk * 2 if a.dtype != BF16 else 0) + (4 << 20))
    return pl.pallas_call(
        functools.partial(_mm_kernel, nk=nk, swiglu=swiglu, norm_width=norm_width, rope_half=rope_half,
                          rope_every=rope_every, out_scale=out_scale),
        grid=(m // bm, n // bn, nk),
        in_specs=in_specs,
        out_specs=pl.BlockSpec((bm, bn), lambda i, j, k: (i, j)),
        out_shape=jax.ShapeDtypeStruct((m, n), out_dtype),
        scratch_shapes=scratch,
        compiler_params=_params(est, 3),
        name="matmul",
    )(*args)


def _dot_nt(q, k):
    return lax.dot_general(q, k, (((1,), (1,)), ((), ())), preferred_element_type=F32)


def _softmax_parts(s_parts):
    m = s_parts[0].max(axis=-1, keepdims=True)
    for s in s_parts[1:]:
        m = jnp.maximum(m, s.max(axis=-1, keepdims=True))
    e_parts = [jnp.exp2(s - m) for s in s_parts]
    denom = e_parts[0].sum(axis=-1, keepdims=True)
    for e in e_parts[1:]:
        denom = denom + e.sum(axis=-1, keepdims=True)
    return e_parts, denom


def _fill_kv(part_refs, kbf, vbf, row0, n_rows):
    k_ref, v_ref = part_refs
    kbf[row0:row0 + n_rows, :] = k_ref[...].astype(BF16)
    vbf[row0:row0 + n_rows, :] = v_ref[...].astype(BF16)


def _fill_mla(part_refs, kbf, vbf, row0, n_rows):
    kv_ref, kpe_ref = part_refs
    d = HEAD_DIM
    kpe = kpe_ref[...].astype(BF16)
    for slot in range(kv_ref.shape[1] // (2 * d)):
        kbf[row0:row0 + n_rows, 2 * slot * d:(2 * slot + 1) * d] = kv_ref[:, 2 * slot * d:(2 * slot + 1) * d]
        kbf[row0:row0 + n_rows, (2 * slot + 1) * d:(2 * slot + 2) * d] = kpe
        vbf[row0:row0 + n_rows, slot * d:(slot + 1) * d] = kv_ref[:, (2 * slot + 1) * d:(2 * slot + 2) * d]


def _attn_kernel(*refs, hpb, kv_share, n_maps, dk, dv, n_parts, refs_per_part, fill, band, lambda_init):
    refs = list(refs)
    if n_maps == 2:
        lam_ref, gain_ref = refs.pop(0), refs.pop(0)
    q_ref = refs.pop(0)
    part_refs = [tuple(refs.pop(0) for _ in range(refs_per_part)) for _ in range(n_parts)]
    bias_ref = refs.pop(0) if band else None
    o_ref = refs.pop(0)
    kbf, vbf = refs
    tq = q_ref.shape[0]
    part_rows = [p[0].shape[0] for p in part_refs]

    @pl.when(pl.program_id(2) == 0)
    def _():
        row0 = 0
        for p, n in zip(part_refs, part_rows):
            fill(p, kbf, vbf, row0, n)
            row0 += n

    if band:
        n_rows, band_rows = band
        wr = min(WIN_R, n_rows)
        r0 = pl.program_id(2) * (tq // GRID_W)
        assert part_rows[0] % GRID_W == 0
        start = part_rows[0] + jnp.clip(r0 - wr // 2, 0, n_rows - band_rows) * GRID_W
        key_rows = [pl.ds(0, part_rows[0]), pl.ds(pl.multiple_of(start, GRID_W), band_rows * GRID_W)]
    else:
        key_rows = [pl.ds(0, sum(part_rows))]

    if n_maps == 2:
        lam = (jnp.exp(jnp.sum(lam_ref[0:1, :] * lam_ref[1:2, :], axis=-1, keepdims=True))
               - jnp.exp(jnp.sum(lam_ref[2:3, :] * lam_ref[3:4, :], axis=-1, keepdims=True)) + lambda_init)

    for hs in range(hpb):
        slot = hs // kv_share
        maps = []
        for mp in range(n_maps):
            q = q_ref[:, (hs * n_maps + mp) * dk:(hs * n_maps + mp + 1) * dk]
            kcols = slice((slot * n_maps + mp) * dk, (slot * n_maps + mp + 1) * dk)
            s_parts = [_dot_nt(q, kbf[rows, kcols]) for rows in key_rows]
            if band:
                s_parts[-1] = s_parts[-1] + bias_ref[hs]
            maps.append(_softmax_parts(s_parts))
        vcols = slice(slot * dv, (slot + 1) * dv)
        if n_maps == 1:
            (e_parts, denom), = maps
            wgts = [e.astype(BF16) for e in e_parts]
        else:
            (e0, l0), (e1, l1) = maps
            c0 = 1.0 / l0
            c1 = lam / l1
            wgts = [(a * c0 - b * c1).astype(BF16) for a, b in zip(e0, e1)]
        o = jnp.dot(wgts[0], vbf[key_rows[0], vcols], preferred_element_type=F32)
        for wgt, rows in zip(wgts[1:], key_rows[1:]):
            o = o + jnp.dot(wgt, vbf[rows, vcols], preferred_element_type=F32)
        if n_maps == 1:
            o = o / denom
        else:
            o = _rms(o, gain_ref[...]) * (1.0 - lambda_init)
        o_ref[:, hs * dv:(hs + 1) * dv] = o.astype(o_ref.dtype)


def _drop_ref(kernel_fn, index):
    def wrapped(*refs):
        return kernel_fn(*refs[:index], *refs[index + 1:])
    return wrapped


def _attention_call(grid, operands, *, t_rows, heads, tq, q_tile_of, key_rows, cfg, name, into=None):
    hpb, dv, n_maps, dk = cfg["hpb"], cfg["dv"], cfg["n_maps"], cfg["dk"]
    slots = hpb // cfg["kv_share"]
    in_specs = [pl.BlockSpec(blk, imap) for _, blk, imap in operands]
    args = [arr for arr, _, _ in operands]
    kernel_fn = functools.partial(_attn_kernel, **cfg)
    aliases = {}
    if into is not None:
        kernel_fn = _drop_ref(kernel_fn, len(args))
        aliases = {len(args): 0}
        in_specs.append(pl.BlockSpec(memory_space=pl.ANY))
        args.append(into)
    est = (hpb * n_maps * tq * key_rows * 10 + 3 * key_rows * slots * (n_maps * dk + dv) * 4
           + 4 * tq * hpb * (n_maps * dk + dv) * 2 + (8 << 20))
    return pl.pallas_call(
        kernel_fn,
        grid=grid,
        in_specs=in_specs,
        out_specs=pl.BlockSpec((tq, hpb * dv), lambda b, g, i: (q_tile_of(b, i), g)),
        out_shape=jax.ShapeDtypeStruct((t_rows, heads * dv), BF16),
        scratch_shapes=[pltpu.VMEM((key_rows, slots * n_maps * dk), BF16),
                        pltpu.VMEM((key_rows, slots * dv), BF16)],
        input_output_aliases=aliases,
        compiler_params=_params(est, 3),
        name=name,
    )(*args)


def _rope_tables(n_ctx_rows, dec_batch, dec_seq, rot_dim):
    t = jnp.arange(dec_seq)
    per_axis = rot_dim // 2
    inv_freq = ROPE_BASE ** (-jnp.arange(0, per_axis, 2, dtype=F32) / per_axis)
    row = (t // GRID_W).astype(F32)[:, None] * inv_freq
    col = (t % GRID_W).astype(F32)[:, None] * inv_freq
    ang = jnp.concatenate([row, col], axis=-1)
    cos, sin = jnp.cos(ang), jnp.sin(ang)
    pad = LANES - rot_dim
    ones = jnp.ones((dec_seq, pad), F32)
    zeros_half = jnp.zeros_like(sin)
    zeros_pad = jnp.zeros((dec_seq, pad), F32)
    if pad == 0:
        tabs = [jnp.concatenate([cos, cos], -1), jnp.concatenate([-sin, sin], -1)]
    else:
        tabs = [jnp.concatenate([cos, cos, ones], -1),
                jnp.concatenate([-sin, zeros_half, zeros_pad], -1),
                jnp.concatenate([zeros_half, sin, zeros_pad], -1)]
    out = []
    for i, tab in enumerate(tabs):
        ctx = jnp.ones((n_ctx_rows, LANES), F32) if i == 0 else jnp.zeros((n_ctx_rows, LANES), F32)
        out.append(jnp.concatenate([ctx, jnp.tile(tab, (dec_batch, 1))], axis=0))
    return out


def _na_band_plan(n_rows, tq):
    wr = min(WIN_R, n_rows)
    rt = tq // GRID_W
    rows_per_lane_tile = LANES // GRID_W
    band_rows = min(n_rows, -(-(wr + rt - 1) // rows_per_lane_tile) * rows_per_lane_tile)
    cases, case_of_tile = [], []
    for qt in range(n_rows // rt):
        r0 = qt * rt
        bs = int(np.clip(r0 - wr // 2, 0, n_rows - band_rows))
        r = r0 + np.arange(rt)[:, None]
        kr = bs + np.arange(band_rows)[None, :]
        rs = np.clip(r - wr // 2, 0, n_rows - wr)
        valid = (kr >= rs) & (kr < rs + wr)
        ridx = np.clip(kr - r + WIN_R - 1, 0, 2 * WIN_R - 2)
        key = (valid.tobytes(), ridx.tobytes())
        for ci, (k0, _, _) in enumerate(cases):
            if k0 == key:
                break
        else:
            ci = len(cases)
            cases.append((key, ridx, valid))
        case_of_tile.append(ci)
    if any(b < a for a, b in zip(case_of_tile, case_of_tile[1:])) or \
            sorted(set(case_of_tile)) != list(range(len(cases))):
        raise NotImplementedError("neighbourhood tile geometries are not monotone in the tile index")
    thresholds = [case_of_tile.index(ci) for ci in range(1, len(cases))]
    ridx = np.stack([c[1] for c in cases])
    valid = np.stack([c[2] for c in cases])
    return band_rows, ridx, valid, thresholds


def _na_bias(rpb, ridx, valid):
    col = jnp.arange(GRID_W)
    col_start = jnp.clip(col - WIN_C // 2, 0, GRID_W - WIN_C)
    in_win = (col[None, :] >= col_start[:, None]) & (col[None, :] < col_start[:, None] + WIN_C)
    col_off = jnp.clip(col[None, :] - col[:, None] + WIN_C - 1, 0, 2 * WIN_C - 2)
    col_bias = jnp.where(in_win[None, None], rpb[:, :, col_off].astype(F32) * LOG2_E, NEG_INF)
    n_case, rt, band_rows = ridx.shape
    blk = jnp.take(col_bias, jnp.asarray(ridx.reshape(-1)), axis=1)
    blk = blk.reshape(rpb.shape[0], n_case, rt, band_rows, GRID_W, GRID_W)
    blk = jnp.where(jnp.asarray(valid)[None, :, :, :, None, None], blk, NEG_INF)
    blk = jnp.transpose(blk, (0, 1, 2, 4, 3, 5))
    return blk.reshape(rpb.shape[0], n_case, rt * GRID_W, band_rows * GRID_W)


def kernel(x_prompt, x_sample, cache_gqa_k, cache_gqa_v, cache_mla_ckv, cache_mla_kpe, cache_diff_k, cache_diff_v, cache_na_k, cache_na_v, c, c_ctx, ada_w, ada_b, norm_mix_pre, norm_mix_post, norm_ffn_pre, norm_ffn_post, ffn_w_gu, ffn_w_down, gqa_w_qkv, gqa_q_norm, gqa_k_norm, gqa_w_o, mla_w_in, mla_q_norm, mla_kv_norm, mla_w_q_up, mla_w_kv_up, mla_w_o, diff_w_qkv, diff_lam_q1, diff_lam_k1, diff_lam_q2, diff_lam_k2, diff_subln, diff_w_o, na_w_qkv, na_rpb, na_w_o):
    nb, seq, d = x_prompt.shape
    db, dseq, _ = x_sample.shape
    past = cache_gqa_k.shape[2]
    depth = ada_w.shape[0]
    tp, ts = nb * seq, db * dseq
    t = tp + ts
    hd = HEAD_DIM
    assert db + 1 <= MOD_ROWS and dseq % GRID_W == 0
    assert tp % dseq == 0 and t % past == 0 and tp % past == 0

    tr = _pick(math.gcd(seq, dseq), 256, 8)
    n_ctx_tiles, tiles_per_dec = tp // tr, dseq // tr

    def seg_of_tile_at(layer):
        return lambda i: layer * MOD_ROWS + jnp.where(i < n_ctx_tiles, 0, 1 + (i - n_ctx_tiles) // tiles_per_dec)

    cond = jnp.concatenate([c_ctx[None, :], c, jnp.zeros((MOD_ROWS - 1 - db, d), F32)], axis=0)
    mods = _adaln(cond, ada_w, ada_b).reshape(depth * MOD_ROWS, 1, 6 * d)

    x = jnp.concatenate([x_prompt.reshape(tp, d), x_sample.reshape(ts, d)], axis=0)

    rope128 = _rope_tables(tp, db, dseq, hd)
    rope64 = _rope_tables(tp, db, dseq, B_ROPE_DIM)
    lat_blk0 = tp // dseq

    def vec(v):
        return v.reshape(1, -1).astype(F32)

    def ctx_out(arr, shape):
        return arr[:tp].reshape((nb, seq) + shape)

    def attention(q, ctx_part, lat_parts, *, heads, kv_share, n_maps, dk, dv, hpb_ctx, hpb_lat, tq_lat, name,
                  fill=_fill_kv, head_operands=(), bias=None, band=None, lambda_init=0.0):
        def part_operands(part, hpb):
            slots = hpb // kv_share
            ops = []
            for arr, rows, blk_of, cols in part:
                if cols is None:
                    ops.append((arr, (rows, LANES), lambda b, g, i, blk_of=blk_of: (blk_of(b), 0)))
                else:
                    ops.append((arr, (rows, slots * cols), lambda b, g, i, blk_of=blk_of: (blk_of(b), g)))
            return ops

        def cfg(hpb, n_parts, band_cfg):
            return dict(hpb=hpb, kv_share=kv_share, n_maps=n_maps, dk=dk, dv=dv, n_parts=n_parts,
                        refs_per_part=len(ctx_part), fill=fill, band=band_cfg, lambda_init=lambda_init)

        qw = n_maps * dk
        hpb_c = min(hpb_ctx, heads)
        ops = [(a, blk, (lambda b, g, i: (0, 0))) for a, blk in head_operands]
        ops.append((q, (seq, hpb_c * qw), lambda b, g, i: (b, g)))
        ops += part_operands(ctx_part, hpb_c)
        o = _attention_call((nb, heads // hpb_c, 1), ops, t_rows=t, heads=heads, tq=seq,
                            q_tile_of=lambda b, i: b, key_rows=seq, cfg=cfg(hpb_c, 1, None), name=name + "_ctx")
        hpb_l = min(hpb_lat, heads)
        n_qt, ctx_tiles = dseq // tq_lat, tp // tq_lat

        def q_tile_of(b, i):
            return ctx_tiles + b * n_qt + i

        ops = [(a, blk, (lambda b, g, i: (0, 0))) for a, blk in head_operands]
        ops.append((q, (tq_lat, hpb_l * qw), lambda b, g, i: (q_tile_of(b, i), g)))
        for part in lat_parts:
            ops += part_operands(part, hpb_l)
        if bias is not None:
            bias_arr, case_of = bias
            ops.append((bias_arr, (hpb_l, None) + bias_arr.shape[2:], lambda b, g, i: (g, case_of(i), 0, 0)))
        return _attention_call((db, heads // hpb_l, n_qt), ops, t_rows=t, heads=heads, tq=tq_lat,
                               q_tile_of=q_tile_of, key_rows=past + dseq, cfg=cfg(hpb_l, len(lat_parts), band),
                               name=name + "_lat", into=o)

    def kv_parts(k, v, ck, cv, cols_k, cols_v):
        ctx = [(k, seq, lambda b: b, cols_k), (v, seq, lambda b: b, cols_v)]
        lat = [[(ck, past, lambda b: b, cols_k), (cv, past, lambda b: b, cols_v)],
               [(k, dseq, lambda b: lat_blk0 + b, cols_k), (v, dseq, lambda b: lat_blk0 + b, cols_v)]]
        return ctx, lat

    outs = {k: [] for k in ("gqa_k", "gqa_v", "mla_ckv", "mla_kpe", "diff_k", "diff_v", "na_k", "na_v")}

    ffn_gu_bf16 = ffn_w_gu.astype(BF16)
    ffn_down_bf16 = ffn_w_down.astype(BF16)

    (h,) = _rows(x, None, mods, tr, pre=(vec(norm_mix_pre[0]), seg_of_tile_at(0), 0, 1))

    for layer in range(depth):
        kind, j = layer % N_MIXERS, layer // N_MIXERS
        seg = seg_of_tile_at(layer)

        if kind == 0:
            w = gqa_w_qkv[j].astype(BF16)
            n_q = gqa_w_o.shape[1]
            n_kv = (w.shape[1] - n_q) // 2
            heads, kv_heads = n_q // hd, n_kv // hd
            group = heads // kv_heads
            q_scale = hd ** -0.5 * LOG2_E
            q = _matmul(h, w, BF16, w_cols=(0, n_q), gain=jnp.tile(vec(gqa_q_norm[j]), (1, heads)), norm_width=hd,
                        rope_tabs=rope128, rope_half=hd // 2, out_scale=q_scale)
            k = _matmul(h, w, F32, w_cols=(n_q, n_kv), gain=jnp.tile(vec(gqa_k_norm[j]), (1, kv_heads)),
                        norm_width=hd, rope_tabs=rope128, rope_half=hd // 2)
            v = _matmul(h, w, F32, w_cols=(n_q + n_kv, n_kv))
            outs["gqa_k"].append(ctx_out(k, (kv_heads, hd)))
            outs["gqa_v"].append(ctx_out(v, (kv_heads, hd)))
            ck = cache_gqa_k[:, j].reshape(db * past, n_kv)
            cv = cache_gqa_v[:, j].reshape(db * past, n_kv)
            ctx_part, lat_parts = kv_parts(k, v, ck, cv, hd, hd)
            o = attention(q, ctx_part, lat_parts, heads=heads, kv_share=group, n_maps=1, dk=hd, dv=hd,
                          hpb_ctx=2 * group, hpb_lat=group, tq_lat=_pick(dseq, 256, GRID_W), name="attn_gqa")
            w_o = gqa_w_o[j]

        elif kind == 1:
            q_rank, kv_rank = mla_q_norm.shape[1], mla_kv_norm.shape[1]
            w_in = mla_w_in[j].astype(BF16)
            rope_dim = w_in.shape[1] - q_rank - kv_rank
            w_kpe = jnp.pad(w_in[:, q_rank + kv_rank:], ((0, 0), (0, LANES - rope_dim)))
            heads = mla_w_o.shape[1] // hd
            q_scale = (hd + rope_dim) ** -0.5 * LOG2_E
            cq = _matmul(h, w_in, BF16, w_cols=(0, q_rank), gain=vec(mla_q_norm[j]), norm_width=q_rank)
            ckv = _matmul(h, w_in, F32, w_cols=(q_rank, kv_rank), gain=vec(mla_kv_norm[j]), norm_width=kv_rank)
            kpe = _matmul(h, w_kpe, F32, rope_tabs=rope64, rope_half=rope_dim // 2)
            outs["mla_ckv"].append(ctx_out(ckv, (kv_rank,)))
            outs["mla_kpe"].append(ctx_out(kpe[:, :rope_dim], (rope_dim,)))
            w_q = jnp.pad(mla_w_q_up[j].astype(BF16).reshape(q_rank, heads, hd + rope_dim),
                          ((0, 0), (0, 0), (0, LANES - rope_dim))).reshape(q_rank, heads * 2 * hd)
            q = _matmul(cq, w_q, BF16, rope_tabs=rope64, rope_half=rope_dim // 2, rope_every=2, out_scale=q_scale)
            ckv_all = jnp.concatenate([ckv, cache_mla_ckv[:, j].reshape(db * past, kv_rank)], axis=0)
            kv = _matmul(ckv_all, mla_w_kv_up[j].astype(BF16), BF16)
            ckpe = jnp.pad(cache_mla_kpe[:, j].reshape(db * past, rope_dim), ((0, 0), (0, LANES - rope_dim)))
            cache_blk0 = t // past
            ctx_part = [(kv, seq, lambda b: b, 2 * hd), (kpe, seq, lambda b: b, None)]
            lat_parts = [[(kv, past, lambda b: cache_blk0 + b, 2 * hd), (ckpe, past, lambda b: b, None)],
                         [(kv, dseq, lambda b: lat_blk0 + b, 2 * hd), (kpe, dseq, lambda b: lat_blk0 + b, None)]]
            o = attention(q, ctx_part, lat_parts, heads=heads, kv_share=1, n_maps=1, dk=2 * hd, dv=hd,
                          hpb_ctx=8, hpb_lat=2, tq_lat=_pick(dseq, 256, GRID_W), name="attn_mla", fill=_fill_mla)
            w_o = mla_w_o[j]

        elif kind == 2:
            lambda_init = 0.8 - 0.6 * math.exp(-0.3 * layer)
            w = diff_w_qkv[j].astype(BF16)
            n_w = w.shape[1] // 3
            heads = n_w // (2 * hd)
            q_scale = hd ** -0.5 * LOG2_E
            q = _matmul(h, w, BF16, w_cols=(0, n_w), rope_tabs=rope128, rope_half=hd // 2, out_scale=q_scale)
            k = _matmul(h, w, F32, w_cols=(n_w, n_w), rope_tabs=rope128, rope_half=hd // 2)
            v = _matmul(h, w, F32, w_cols=(2 * n_w, n_w))
            outs["diff_k"].append(ctx_out(k, (heads, 2, hd)))
            outs["diff_v"].append(ctx_out(v, (heads, 2 * hd)))
            ck = cache_diff_k[:, j].reshape(db * past, n_w)
            cv = cache_diff_v[:, j].reshape(db * past, n_w)
            lam = jnp.stack([diff_lam_q1[j], diff_lam_k1[j], diff_lam_q2[j], diff_lam_k2[j]]).astype(F32)
            ctx_part, lat_parts = kv_parts(k, v, ck, cv, 2 * hd, 2 * hd)
            o = attention(q, ctx_part, lat_parts, heads=heads, kv_share=1, n_maps=2, dk=hd, dv=2 * hd,
                          hpb_ctx=4, hpb_lat=2, tq_lat=_pick(dseq, 256, GRID_W), name="attn_diff",
                          head_operands=[(lam, (4, hd)), (vec(diff_subln[j]), (1, 2 * hd))], lambda_init=lambda_init)
            w_o = diff_w_o[j]

        else:
            w = na_w_qkv[j].astype(BF16)
            n_w = w.shape[1] // 3
            heads = n_w // hd
            q_scale = hd ** -0.5 * LOG2_E
            q = _matmul(h, w, BF16, w_cols=(0, n_w), out_scale=q_scale)
            k = _matmul(h, w, F32, w_cols=(n_w, n_w))
            v = _matmul(h, w, F32, w_cols=(2 * n_w, n_w))
            outs["na_k"].append(ctx_out(k, (heads, hd)))
            outs["na_v"].append(ctx_out(v, (heads, hd)))
            ck = cache_na_k[:, j].reshape(db * past, n_w)
            cv = cache_na_v[:, j].reshape(db * past, n_w)
            n_rows = dseq // GRID_W
            tq_na = _pick(dseq, 256, GRID_W)
            band_rows, ridx, valid, thresholds = _na_band_plan(n_rows, tq_na)
            bias = _na_bias(na_rpb[j], ridx, valid)

            def case_of(i):
                return sum((i >= th).astype(jnp.int32) for th in thresholds) if thresholds else 0

            ctx_part, lat_parts = kv_parts(k, v, ck, cv, hd, hd)
            o = attention(q, ctx_part, lat_parts, heads=heads, kv_share=1, n_maps=1, dk=hd, dv=hd,
                          hpb_ctx=8, hpb_lat=4, tq_lat=tq_na, name="attn_na", bias=(bias, case_of),
                          band=(n_rows, band_rows))
            w_o = na_w_o[j]

        y = _matmul(o, w_o.astype(BF16), F32)
        x, h = _rows(x, y, mods, tr, post=(vec(norm_mix_post[layer]), seg, 2),
                     pre=(vec(norm_ffn_pre[layer]), seg, 3, 4))

        act = _matmul(h, ffn_gu_bf16, BF16, layer=layer, swiglu=True, bn=256)
        y = _matmul(act, ffn_down_bf16, F32, layer=layer, bk=ffn_w_down.shape[1] // 2)
        post = (vec(norm_ffn_post[layer]), seg, 5)
        if layer + 1 < depth:
            x, h = _rows(x, y, mods, tr, post=post,
                         pre=(vec(norm_mix_pre[layer + 1]), seg_of_tile_at(layer + 1), 0, 1))
        else:
            (y_prompt,) = _rows(x, y, mods, tr, post=post, tile0=0, n_tiles=n_ctx_tiles)
            (y_sample,) = _rows(x, y, mods, tr, post=post, tile0=n_ctx_tiles, n_tiles=ts // tr)

    return (y_prompt.reshape(nb, seq, d), y_sample.reshape(db, dseq, d),
            jnp.stack(outs["gqa_k"], axis=1), jnp.stack(outs["gqa_v"], axis=1),
            jnp.stack(outs["mla_ckv"], axis=1), jnp.stack(outs["mla_kpe"], axis=1),
            jnp.stack(outs["diff_k"], axis=1), jnp.stack(outs["diff_v"], axis=1),
            jnp.stack(outs["na_k"], axis=1), jnp.stack(outs["na_v"], axis=1))
```

```python
import functools
import math

import numpy as np
import jax
import jax.numpy as jnp
from jax import lax
from jax.experimental import pallas as pl
from jax.experimental.pallas import tpu as pltpu

GRID_W = 64
N_MIXERS = 4
ROPE_BASE = 10000.0
NORM_EPS = 1e-6
NEG_INF = -1e30
HEAD_DIM = 128
B_ROPE_DIM = 64
WIN_R = 8
WIN_C = 16
LOG2_E = math.log2(math.e)

LANES = 128
MOD_ROWS = 16
V7X_VMEM_BYTES = 64 * 1024 * 1024
VMEM_CAP_BYTES = V7X_VMEM_BYTES - 8 * 1024 * 1024

BF16 = jnp.bfloat16
F32 = jnp.float32


def _pick(n, target, mult):
    best = None
    for t in range(mult, min(n, target) + 1, mult):
        if n % t == 0:
            best = t
    return n if best is None else best


def _params(vmem_estimate_bytes, n_grid):
    limit = int(min(VMEM_CAP_BYTES, max(32 * 1024 * 1024, vmem_estimate_bytes)))
    return pltpu.CompilerParams(dimension_semantics=("arbitrary",) * n_grid, vmem_limit_bytes=limit)


def _adaln_kernel(c_ref, w_ref, b_ref, o_ref):
    c = c_ref[...]
    a = (c * jax.nn.sigmoid(c)).astype(BF16)
    o_ref[...] = jnp.dot(a, w_ref[...].astype(BF16), preferred_element_type=F32) + b_ref[...]


def _adaln(cond, ada_w, ada_b):
    depth, d, n = ada_w.shape
    tn = _pick(n, 512, LANES)
    est = 2 * d * tn * 4 + d * tn * 2 + 4 * MOD_ROWS * (d + tn) * 4
    return pl.pallas_call(
        _adaln_kernel,
        grid=(depth, n // tn),
        in_specs=[
            pl.BlockSpec((MOD_ROWS, d), lambda l, j: (0, 0)),
            pl.BlockSpec((None, d, tn), lambda l, j: (l, 0, j)),
            pl.BlockSpec((None, 1, tn), lambda l, j: (l, 0, j)),
        ],
        out_specs=pl.BlockSpec((None, MOD_ROWS, tn), lambda l, j: (l, 0, j)),
        out_shape=jax.ShapeDtypeStruct((depth, MOD_ROWS, n), F32),
        compiler_params=_params(est + (8 << 20), 2),
        name="adaln",
    )(cond, ada_w, ada_b.reshape(depth, 1, n))


def _rms(x, g):
    return x * lax.rsqrt(jnp.mean(x * x, axis=-1, keepdims=True) + NORM_EPS) * g


def _rows_kernel(*refs, has_post, has_pre):
    refs = list(refs)
    x = refs.pop(0)[...]
    if has_post:
        y_ref, gpost_ref, gate_ref = refs[:3]
        refs = refs[3:]
        x = x + gate_ref[0] * _rms(y_ref[...], gpost_ref[...])
    if has_pre:
        gpre_ref, shift_ref, scale_ref = refs[:3]
        refs = refs[3:]
    if has_post:
        refs.pop(0)[...] = x
    if has_pre:
        refs.pop(0)[...] = (_rms(x, gpre_ref[...]) * (1.0 + scale_ref[0]) + shift_ref[0]).astype(BF16)


def _rows(x, y, mods, tr, post=None, pre=None, tile0=0, n_tiles=None):
    t, d = x.shape
    n_tiles = t // tr if n_tiles is None else n_tiles
    row_spec = pl.BlockSpec((tr, d), lambda i: (i + tile0, 0))
    out_spec = pl.BlockSpec((tr, d), lambda i: (i, 0))
    vec_spec = pl.BlockSpec((1, d), lambda i: (0, 0))

    def mod_spec(seg_of_tile, slot):
        return pl.BlockSpec((1, 1, d), lambda i: (seg_of_tile(i + tile0), 0, slot))

    args, in_specs, out_shape, out_specs = [x], [row_spec], [], []
    if post is not None:
        args += [y, post[0], mods]
        in_specs += [row_spec, vec_spec, mod_spec(post[1], post[2])]
        out_shape.append(jax.ShapeDtypeStruct((n_tiles * tr, d), F32))
        out_specs.append(out_spec)
    if pre is not None:
        args += [pre[0], mods, mods]
        in_specs += [vec_spec, mod_spec(pre[1], pre[2]), mod_spec(pre[1], pre[3])]
        out_shape.append(jax.ShapeDtypeStruct((n_tiles * tr, d), BF16))
        out_specs.append(out_spec)
    est = 2 * tr * d * 4 * (len(args) + 2) + (8 << 20)
    out = pl.pallas_call(
        functools.partial(_rows_kernel, has_post=post is not None, has_pre=pre is not None),
        grid=(n_tiles,),
        in_specs=in_specs,
        out_specs=out_specs,
        out_shape=out_shape,
        compiler_params=_params(est, 1),
        name="rows",
    )(*args)
    return out


def _rope_lanes(y, tabs, half):
    if half == LANES // 2:
        return y * tabs[0] + pltpu.roll(y, LANES // 2, 1) * tabs[1]
    return (y * tabs[0] + pltpu.roll(y, LANES - half, 1) * tabs[1] + pltpu.roll(y, half, 1) * tabs[2])


def _mm_kernel(*refs, nk, swiglu, norm_width, rope_half, rope_every, out_scale):
    refs = list(refs)
    a_ref = refs.pop(0)
    w_ref = refs.pop(0)
    w2_ref = refs.pop(0) if swiglu else None
    gain_ref = refs.pop(0) if norm_width else None
    n_tab = 0 if not rope_half else (2 if rope_half == LANES // 2 else 3)
    tab_refs = [refs.pop(0) for _ in range(n_tab)]
    o_ref = refs.pop(0)
    acc_refs = refs

    a = a_ref[...].astype(BF16)
    parts = [jnp.dot(a, w_ref[...].astype(BF16), preferred_element_type=F32)]
    if swiglu:
        parts.append(jnp.dot(a, w2_ref[...].astype(BF16), preferred_element_type=F32))

    def epilogue(vals):
        y = vals[0]
        if swiglu:
            y = (y * jax.nn.sigmoid(y)) * vals[1]
        bn = y.shape[-1]
        if norm_width or rope_half:
            tabs = [r[...] for r in tab_refs]
            width = norm_width if norm_width else LANES
            for c in range(bn // width):
                sl = slice(c * width, (c + 1) * width)
                yc = y[:, sl]
                if norm_width:
                    yc = _rms(yc, gain_ref[:, sl])
                if rope_half and c % rope_every == rope_every - 1:
                    yc = _rope_lanes(yc, tabs, rope_half)
                if out_scale != 1.0:
                    yc = yc * out_scale
                o_ref[:, sl] = yc.astype(o_ref.dtype)
        else:
            if out_scale != 1.0:
                y = y * out_scale
            o_ref[...] = y.astype(o_ref.dtype)

    if nk == 1:
        epilogue(parts)
    else:
        k = pl.program_id(2)

        @pl.when(k == 0)
        def _():
            for r, p in zip(acc_refs, parts):
                r[...] = p

        @pl.when(k > 0)
        def _():
            for r, p in zip(acc_refs, parts):
                r[...] += p

        @pl.when(k == nk - 1)
        def _():
            epilogue([r[...] for r in acc_refs])


def _matmul(a, w, out_dtype, *, layer=0, w_cols=None, swiglu=False, gain=None, norm_width=0, rope_tabs=None,
            rope_half=0, rope_every=1, out_scale=1.0, bm=1024, bn=None, bk=None, single_buffer_a=False):
    m, kdim = a.shape
    col0, n = (0, w.shape[-1]) if w_cols is None else w_cols
    if swiglu:
        col0, n = 0, w.shape[-1] // 2
    bm = _pick(m, bm, 8)
    if bn is None:
        bn = 1024 if not (swiglu or norm_width or rope_half or bk) else 512
    bn = _pick(math.gcd(n, col0) if col0 else n, bn, LANES * rope_every)
    if norm_width > LANES:
        bn = norm_width
    bk = kdim if bk is None else _pick(kdim, bk, LANES)
    nk = kdim // bk
    jb = col0 // bn

    def w_spec(col_blk0):
        if w.ndim == 3:
            return pl.BlockSpec((None, bk, bn), lambda i, j, k: (layer, k, j + col_blk0))
        return pl.BlockSpec((bk, bn), lambda i, j, k: (k, j + col_blk0))

    a_buffers = 1 if (single_buffer_a and nk == 1) else 2
    a_mode = dict(pipeline_mode=pl.Buffered(1)) if a_buffers == 1 else {}
    args = [a, w]
    in_specs = [pl.BlockSpec((bm, bk), lambda i, j, k: (i, k), **a_mode), w_spec(jb)]
    if swiglu:
        args.append(w)
        in_specs.append(w_spec(n // bn))
    if norm_width:
        args.append(gain)
        in_specs.append(pl.BlockSpec((1, bn), lambda i, j, k: (0, j)))
    if rope_half:
        for tab in rope_tabs:
            args.append(tab)
            in_specs.append(pl.BlockSpec((bm, LANES), lambda i, j, k: (i, 0)))
    n_w = 2 if swiglu else 1
    scratch = [pltpu.VMEM((bm, bn), F32) for _ in range(n_w)] if nk > 1 else []
    est = (a_buffers * bm * bk * a.dtype.itemsize + (bm * bk * 2 if a.dtype != BF16 else 0)
           + 2 * n_w * bk * bn * w.dtype.itemsize + (n_w * bk * bn * 2 if w.dtype != BF16 else 0)
           + 2 * bm * bn * jnp.dtype(out_dtype).itemsize + (n_w + 2) * bm * bn * 4 + (4 << 20))
    return pl.pallas_call(
        functools.partial(_mm_kernel, nk=nk, swiglu=swiglu, norm_width=norm_width, rope_half=rope_half,
                          rope_every=rope_every, out_scale=out_scale),
        grid=(m // bm, n // bn, nk),
        in_specs=in_specs,
        out_specs=pl.BlockSpec((bm, bn), lambda i, j, k: (i, j)),
        out_shape=jax.ShapeDtypeStruct((m, n), out_dtype),
        scratch_shapes=scratch,
        compiler_params=_params(est, 3),
        name="matmul",
    )(*args)


def _dot_nt(q, k):
    return lax.dot_general(q, k, (((1,), (1,)), ((), ())), preferred_element_type=F32)


def _softmax_parts(s_parts):
    m = s_parts[0].max(axis=-1, keepdims=True)
    for s in s_parts[1:]:
        m = jnp.maximum(m, s.max(axis=-1, keepdims=True))
    e_parts = [jnp.exp2(s - m) for s in s_parts]
    denom = e_parts[0].sum(axis=-1, keepdims=True)
    for e in e_parts[1:]:
        denom = denom + e.sum(axis=-1, keepdims=True)
    return e_parts, denom


def _fill_kv(part_refs, kbf, vbf, row0, n_rows):
    k_ref, v_ref = part_refs
    kbf[row0:row0 + n_rows, :] = k_ref[...].astype(BF16)
    vbf[row0:row0 + n_rows, :] = v_ref[...].astype(BF16)


def _fill_mla(part_refs, kbf, vbf, row0, n_rows):
    kv_ref, kpe_ref = part_refs
    d = HEAD_DIM
    kpe = kpe_ref[...].astype(BF16)
    for slot in range(kv_ref.shape[1] // (2 * d)):
        kbf[row0:row0 + n_rows, 2 * slot * d:(2 * slot + 1) * d] = kv_ref[:, 2 * slot * d:(2 * slot + 1) * d]
        kbf[row0:row0 + n_rows, (2 * slot + 1) * d:(2 * slot + 2) * d] = kpe
        vbf[row0:row0 + n_rows, slot * d:(slot + 1) * d] = kv_ref[:, (2 * slot + 1) * d:(2 * slot + 2) * d]


def _attn_kernel(*refs, hpb, kv_share, n_maps, dk, dv, n_parts, refs_per_part, fill, band, lambda_init):
    refs = list(refs)
    if n_maps == 2:
        lam_ref, gain_ref = refs.pop(0), refs.pop(0)
    q_ref = refs.pop(0)
    part_refs = [tuple(refs.pop(0) for _ in range(refs_per_part)) for _ in range(n_parts)]
    bias_ref = refs.pop(0) if band else None
    o_ref = refs.pop(0)
    kbf, vbf = refs
    tq = q_ref.shape[0]
    part_rows = [p[0].shape[0] for p in part_refs]

    @pl.when(pl.program_id(2) == 0)
    def _():
        row0 = 0
        for p, n in zip(part_refs, part_rows):
            fill(p, kbf, vbf, row0, n)
            row0 += n

    if band:
        n_rows, band_rows = band
        wr = min(WIN_R, n_rows)
        r0 = pl.program_id(2) * (tq // GRID_W)
        assert part_rows[0] % GRID_W == 0
        start = part_rows[0] + jnp.clip(r0 - wr // 2, 0, n_rows - band_rows) * GRID_W
        key_rows = [pl.ds(0, part_rows[0]), pl.ds(pl.multiple_of(start, GRID_W), band_rows * GRID_W)]
    else:
        key_rows = [pl.ds(0, sum(part_rows))]

    if n_maps == 2:
        lam = (jnp.exp(jnp.sum(lam_ref[0:1, :] * lam_ref[1:2, :], axis=-1, keepdims=True))
               - jnp.exp(jnp.sum(lam_ref[2:3, :] * lam_ref[3:4, :], axis=-1, keepdims=True)) + lambda_init)

    for hs in range(hpb):
        slot = hs // kv_share
        maps = []
        for mp in range(n_maps):
            q = q_ref[:, (hs * n_maps + mp) * dk:(hs * n_maps + mp + 1) * dk]
            kcols = slice((slot * n_maps + mp) * dk, (slot * n_maps + mp + 1) * dk)
            s_parts = [_dot_nt(q, kbf[rows, kcols]) for rows in key_rows]
            if band:
                s_parts[-1] = s_parts[-1] + bias_ref[hs]
            maps.append(_softmax_parts(s_parts))
        vcols = slice(slot * dv, (slot + 1) * dv)
        if n_maps == 1:
            (e_parts, denom), = maps
            wgts = [e.astype(BF16) for e in e_parts]
        else:
            (e0, l0), (e1, l1) = maps
            ratio = lam * l0 / l1
            wgts = [(a - b * ratio).astype(BF16) for a, b in zip(e0, e1)]
            denom = l0
        o = jnp.dot(wgts[0], vbf[key_rows[0], vcols], preferred_element_type=F32)
        for wgt, rows in zip(wgts[1:], key_rows[1:]):
            o = o + jnp.dot(wgt, vbf[rows, vcols], preferred_element_type=F32)
        o = o / denom
        if n_maps == 2:
            o = _rms(o, gain_ref[...]) * (1.0 - lambda_init)
        o_ref[:, hs * dv:(hs + 1) * dv] = o.astype(o_ref.dtype)


def _drop_ref(kernel_fn, index):
    def wrapped(*refs):
        return kernel_fn(*refs[:index], *refs[index + 1:])
    return wrapped


def _attention_call(grid, operands, *, t_rows, heads, tq, q_tile_of, key_rows, cfg, name, into=None):
    hpb, dv, n_maps, dk = cfg["hpb"], cfg["dv"], cfg["n_maps"], cfg["dk"]
    slots = hpb // cfg["kv_share"]
    in_specs = [pl.BlockSpec(blk, imap) for _, blk, imap in operands]
    args = [arr for arr, _, _ in operands]
    kernel_fn = functools.partial(_attn_kernel, **cfg)
    aliases = {}
    if into is not None:
        kernel_fn = _drop_ref(kernel_fn, len(args))
        aliases = {len(args): 0}
        in_specs.append(pl.BlockSpec(memory_space=pl.ANY))
        args.append(into)
    est = (hpb * n_maps * tq * key_rows * 10 + 3 * key_rows * slots * (n_maps * dk + dv) * 4
           + 4 * tq * hpb * (n_maps * dk + dv) * 2 + (8 << 20))
    return pl.pallas_call(
        kernel_fn,
        grid=grid,
        in_specs=in_specs,
        out_specs=pl.BlockSpec((tq, hpb * dv), lambda b, g, i: (q_tile_of(b, i), g)),
        out_shape=jax.ShapeDtypeStruct((t_rows, heads * dv), BF16),
        scratch_shapes=[pltpu.VMEM((key_rows, slots * n_maps * dk), BF16),
                        pltpu.VMEM((key_rows, slots * dv), BF16)],
        input_output_aliases=aliases,
        compiler_params=_params(est, 3),
        name=name,
    )(*args)


def _rope_tables(n_ctx_rows, dec_batch, dec_seq, rot_dim):
    t = jnp.arange(dec_seq)
    per_axis = rot_dim // 2
    inv_freq = ROPE_BASE ** (-jnp.arange(0, per_axis, 2, dtype=F32) / per_axis)
    row = (t // GRID_W).astype(F32)[:, None] * inv_freq
    col = (t % GRID_W).astype(F32)[:, None] * inv_freq
    ang = jnp.concatenate([row, col], axis=-1)
    cos, sin = jnp.cos(ang), jnp.sin(ang)
    pad = LANES - rot_dim
    ones = jnp.ones((dec_seq, pad), F32)
    zeros_half = jnp.zeros_like(sin)
    zeros_pad = jnp.zeros((dec_seq, pad), F32)
    if pad == 0:
        tabs = [jnp.concatenate([cos, cos], -1), jnp.concatenate([-sin, sin], -1)]
    else:
        tabs = [jnp.concatenate([cos, cos, ones], -1),
                jnp.concatenate([-sin, zeros_half, zeros_pad], -1),
                jnp.concatenate([zeros_half, sin, zeros_pad], -1)]
    out = []
    for i, tab in enumerate(tabs):
        ctx = jnp.ones((n_ctx_rows, LANES), F32) if i == 0 else jnp.zeros((n_ctx_rows, LANES), F32)
        out.append(jnp.concatenate([ctx, jnp.tile(tab, (dec_batch, 1))], axis=0))
    return out


def _na_band_plan(n_rows, tq):
    wr = min(WIN_R, n_rows)
    rt = tq // GRID_W
    rows_per_lane_tile = LANES // GRID_W
    band_rows = min(n_rows, -(-(wr + rt - 1) // rows_per_lane_tile) * rows_per_lane_tile)
    cases, case_of_tile = [], []
    for qt in range(n_rows // rt):
        r0 = qt * rt
        bs = int(np.clip(r0 - wr // 2, 0, n_rows - band_rows))
        r = r0 + np.arange(rt)[:, None]
        kr = bs + np.arange(band_rows)[None, :]
        rs = np.clip(r - wr // 2, 0, n_rows - wr)
        valid = (kr >= rs) & (kr < rs + wr)
        ridx = np.clip(kr - r + WIN_R - 1, 0, 2 * WIN_R - 2)
        key = (valid.tobytes(), ridx.tobytes())
        for ci, (k0, _, _) in enumerate(cases):
            if k0 == key:
                break
        else:
            ci = len(cases)
            cases.append((key, ridx, valid))
        case_of_tile.append(ci)
    if any(b < a for a, b in zip(case_of_tile, case_of_tile[1:])) or \
            sorted(set(case_of_tile)) != list(range(len(cases))):
        raise NotImplementedError("neighbourhood tile geometries are not monotone in the tile index")
    thresholds = [case_of_tile.index(ci) for ci in range(1, len(cases))]
    ridx = np.stack([c[1] for c in cases])
    valid = np.stack([c[2] for c in cases])
    return band_rows, ridx, valid, thresholds


def _na_bias(rpb, ridx, valid):
    col = jnp.arange(GRID_W)
    col_start = jnp.clip(col - WIN_C // 2, 0, GRID_W - WIN_C)
    in_win = (col[None, :] >= col_start[:, None]) & (col[None, :] < col_start[:, None] + WIN_C)
    col_off = jnp.clip(col[None, :] - col[:, None] + WIN_C - 1, 0, 2 * WIN_C - 2)
    col_bias = jnp.where(in_win[None, None], rpb[:, :, col_off].astype(F32) * LOG2_E, NEG_INF)
    n_case, rt, band_rows = ridx.shape
    blk = jnp.take(col_bias, jnp.asarray(ridx.reshape(-1)), axis=1)
    blk = blk.reshape(rpb.shape[0], n_case, rt, band_rows, GRID_W, GRID_W)
    blk = jnp.where(jnp.asarray(valid)[None, :, :, :, None, None], blk, NEG_INF)
    blk = jnp.transpose(blk, (0, 1, 2, 4, 3, 5))
    return blk.reshape(rpb.shape[0], n_case, rt * GRID_W, band_rows * GRID_W)


def kernel(x_prompt, x_sample, cache_gqa_k, cache_gqa_v, cache_mla_ckv, cache_mla_kpe, cache_diff_k, cache_diff_v, cache_na_k, cache_na_v, c, c_ctx, ada_w, ada_b, norm_mix_pre, norm_mix_post, norm_ffn_pre, norm_ffn_post, ffn_w_gu, ffn_w_down, gqa_w_qkv, gqa_q_norm, gqa_k_norm, gqa_w_o, mla_w_in, mla_q_norm, mla_kv_norm, mla_w_q_up, mla_w_kv_up, mla_w_o, diff_w_qkv, diff_lam_q1, diff_lam_k1, diff_lam_q2, diff_lam_k2, diff_subln, diff_w_o, na_w_qkv, na_rpb, na_w_o):
    nb, seq, d = x_prompt.shape
    db, dseq, _ = x_sample.shape
    past = cache_gqa_k.shape[2]
    depth = ada_w.shape[0]
    tp, ts = nb * seq, db * dseq
    t = tp + ts
    hd = HEAD_DIM
    assert db + 1 <= MOD_ROWS and dseq % GRID_W == 0
    assert tp % dseq == 0 and t % past == 0 and tp % past == 0

    tr = _pick(math.gcd(seq, dseq), 256, 8)
    n_ctx_tiles, tiles_per_dec = tp // tr, dseq // tr

    def seg_of_tile_at(layer):
        return lambda i: layer * MOD_ROWS + jnp.where(i < n_ctx_tiles, 0, 1 + (i - n_ctx_tiles) // tiles_per_dec)

    cond = jnp.concatenate([c_ctx[None, :], c, jnp.zeros((MOD_ROWS - 1 - db, d), F32)], axis=0)
    mods = _adaln(cond, ada_w, ada_b).reshape(depth * MOD_ROWS, 1, 6 * d)

    x = jnp.concatenate([x_prompt.reshape(tp, d), x_sample.reshape(ts, d)], axis=0)

    rope128 = _rope_tables(tp, db, dseq, hd)
    rope64 = _rope_tables(tp, db, dseq, B_ROPE_DIM)
    lat_blk0 = tp // dseq

    def vec(v):
        return v.reshape(1, -1).astype(F32)

    def ctx_out(arr, shape):
        return arr[:tp].reshape((nb, seq) + shape)

    def attention(q, ctx_part, lat_parts, *, heads, kv_share, n_maps, dk, dv, hpb_ctx, hpb_lat, tq_lat, name,
                  fill=_fill_kv, head_operands=(), bias=None, band=None, lambda_init=0.0):
        def part_operands(part, hpb):
            slots = hpb // kv_share
            ops = []
            for arr, rows, blk_of, cols in part:
                if cols is None:
                    ops.append((arr, (rows, LANES), lambda b, g, i, blk_of=blk_of: (blk_of(b), 0)))
                else:
                    ops.append((arr, (rows, slots * cols), lambda b, g, i, blk_of=blk_of: (blk_of(b), g)))
            return ops

        def cfg(hpb, n_parts, band_cfg):
            return dict(hpb=hpb, kv_share=kv_share, n_maps=n_maps, dk=dk, dv=dv, n_parts=n_parts,
                        refs_per_part=len(ctx_part), fill=fill, band=band_cfg, lambda_init=lambda_init)

        qw = n_maps * dk
        hpb_c = min(hpb_ctx, heads)
        ops = [(a, blk, (lambda b, g, i: (0, 0))) for a, blk in head_operands]
        ops.append((q, (seq, hpb_c * qw), lambda b, g, i: (b, g)))
        ops += part_operands(ctx_part, hpb_c)
        o = _attention_call((nb, heads // hpb_c, 1), ops, t_rows=t, heads=heads, tq=seq,
                            q_tile_of=lambda b, i: b, key_rows=seq, cfg=cfg(hpb_c, 1, None), name=name + "_ctx",
                            into=jnp.zeros((t, heads * dv), BF16))
        hpb_l = min(hpb_lat, heads)
        n_qt, ctx_tiles = dseq // tq_lat, tp // tq_lat

        def q_tile_of(b, i):
            return ctx_tiles + b * n_qt + i

        ops = [(a, blk, (lambda b, g, i: (0, 0))) for a, blk in head_operands]
        ops.append((q, (tq_lat, hpb_l * qw), lambda b, g, i: (q_tile_of(b, i), g)))
        for part in lat_parts:
            ops += part_operands(part, hpb_l)
        if bias is not None:
            bias_arr, case_of = bias
            ops.append((bias_arr, (hpb_l, None) + bias_arr.shape[2:], lambda b, g, i: (g, case_of(i), 0, 0)))
        return _attention_call((db, heads // hpb_l, n_qt), ops, t_rows=t, heads=heads, tq=tq_lat,
                               q_tile_of=q_tile_of, key_rows=past + dseq, cfg=cfg(hpb_l, len(lat_parts), band),
                               name=name + "_lat", into=o)

    def kv_parts(k, v, ck, cv, cols_k, cols_v):
        ctx = [(k, seq, lambda b: b, cols_k), (v, seq, lambda b: b, cols_v)]
        lat = [[(ck, past, lambda b: b, cols_k), (cv, past, lambda b: b, cols_v)],
               [(k, dseq, lambda b: lat_blk0 + b, cols_k), (v, dseq, lambda b: lat_blk0 + b, cols_v)]]
        return ctx, lat

    outs = {k: [] for k in ("gqa_k", "gqa_v", "mla_ckv", "mla_kpe", "diff_k", "diff_v", "na_k", "na_v")}

    ffn_down_bf16 = ffn_w_down.astype(BF16)

    (h,) = _rows(x, None, mods, tr, pre=(vec(norm_mix_pre[0]), seg_of_tile_at(0), 0, 1))

    for layer in range(depth):
        kind, j = layer % N_MIXERS, layer // N_MIXERS
        seg = seg_of_tile_at(layer)

        if kind == 0:
            w = gqa_w_qkv[j].astype(BF16)
            n_q = gqa_w_o.shape[1]
            n_kv = (w.shape[1] - n_q) // 2
            heads, kv_heads = n_q // hd, n_kv // hd
            group = heads // kv_heads
            q_scale = hd ** -0.5 * LOG2_E
            q = _matmul(h, w, BF16, w_cols=(0, n_q), gain=jnp.tile(vec(gqa_q_norm[j]), (1, heads)), norm_width=hd,
                        rope_tabs=rope128, rope_half=hd // 2, out_scale=q_scale)
            k = _matmul(h, w, F32, w_cols=(n_q, n_kv), gain=jnp.tile(vec(gqa_k_norm[j]), (1, kv_heads)),
                        norm_width=hd, rope_tabs=rope128, rope_half=hd // 2)
            v = _matmul(h, w, F32, w_cols=(n_q + n_kv, n_kv))
            outs["gqa_k"].append(ctx_out(k, (kv_heads, hd)))
            outs["gqa_v"].append(ctx_out(v, (kv_heads, hd)))
            ck = cache_gqa_k[:, j].reshape(db * past, n_kv)
            cv = cache_gqa_v[:, j].reshape(db * past, n_kv)
            ctx_part, lat_parts = kv_parts(k, v, ck, cv, hd, hd)
            o = attention(q, ctx_part, lat_parts, heads=heads, kv_share=group, n_maps=1, dk=hd, dv=hd,
                          hpb_ctx=2 * group, hpb_lat=group, tq_lat=_pick(dseq, 256, GRID_W), name="attn_gqa")
            w_o = gqa_w_o[j]

        elif kind == 1:
            q_rank, kv_rank = mla_q_norm.shape[1], mla_kv_norm.shape[1]
            w_in = mla_w_in[j].astype(BF16)
            rope_dim = w_in.shape[1] - q_rank - kv_rank
            w_kpe = jnp.pad(w_in[:, q_rank + kv_rank:], ((0, 0), (0, LANES - rope_dim)))
            heads = mla_w_o.shape[1] // hd
            q_scale = (hd + rope_dim) ** -0.5 * LOG2_E
            cq = _matmul(h, w_in, BF16, w_cols=(0, q_rank), gain=vec(mla_q_norm[j]), norm_width=q_rank)
            ckv = _matmul(h, w_in, F32, w_cols=(q_rank, kv_rank), gain=vec(mla_kv_norm[j]), norm_width=kv_rank)
            kpe = _matmul(h, w_kpe, F32, rope_tabs=rope64, rope_half=rope_dim // 2)
            outs["mla_ckv"].append(ctx_out(ckv, (kv_rank,)))
            outs["mla_kpe"].append(ctx_out(kpe[:, :rope_dim], (rope_dim,)))
            w_q = jnp.pad(mla_w_q_up[j].astype(BF16).reshape(q_rank, heads, hd + rope_dim),
                          ((0, 0), (0, 0), (0, LANES - rope_dim))).reshape(q_rank, heads * 2 * hd)
            q = _matmul(cq, w_q, BF16, rope_tabs=rope64, rope_half=rope_dim // 2, rope_every=2, out_scale=q_scale)
            ckv_all = jnp.concatenate([ckv, cache_mla_ckv[:, j].reshape(db * past, kv_rank)], axis=0)
            kv = _matmul(ckv_all, mla_w_kv_up[j].astype(BF16), BF16)
            ckpe = jnp.pad(cache_mla_kpe[:, j].reshape(db * past, rope_dim), ((0, 0), (0, LANES - rope_dim)))
            cache_blk0 = t // past
            ctx_part = [(kv, seq, lambda b: b, 2 * hd), (kpe, seq, lambda b: b, None)]
            lat_parts = [[(kv, past, lambda b: cache_blk0 + b, 2 * hd), (ckpe, past, lambda b: b, None)],
                         [(kv, dseq, lambda b: lat_blk0 + b, 2 * hd), (kpe, dseq, lambda b: lat_blk0 + b, None)]]
            o = attention(q, ctx_part, lat_parts, heads=heads, kv_share=1, n_maps=1, dk=2 * hd, dv=hd,
                          hpb_ctx=8, hpb_lat=2, tq_lat=_pick(dseq, 256, GRID_W), name="attn_mla", fill=_fill_mla)
            w_o = mla_w_o[j]

        elif kind == 2:
            lambda_init = 0.8 - 0.6 * math.exp(-0.3 * layer)
            w = diff_w_qkv[j].astype(BF16)
            n_w = w.shape[1] // 3
            heads = n_w // (2 * hd)
            q_scale = hd ** -0.5 * LOG2_E
            q = _matmul(h, w, BF16, w_cols=(0, n_w), rope_tabs=rope128, rope_half=hd // 2, out_scale=q_scale)
            k = _matmul(h, w, F32, w_cols=(n_w, n_w), rope_tabs=rope128, rope_half=hd // 2)
            v = _matmul(h, w, F32, w_cols=(2 * n_w, n_w))
            outs["diff_k"].append(ctx_out(k, (heads, 2, hd)))
            outs["diff_v"].append(ctx_out(v, (heads, 2 * hd)))
            ck = cache_diff_k[:, j].reshape(db * past, n_w)
            cv = cache_diff_v[:, j].reshape(db * past, n_w)
            lam = jnp.stack([diff_lam_q1[j], diff_lam_k1[j], diff_lam_q2[j], diff_lam_k2[j]]).astype(F32)
            ctx_part, lat_parts = kv_parts(k, v, ck, cv, 2 * hd, 2 * hd)
            o = attention(q, ctx_part, lat_parts, heads=heads, kv_share=1, n_maps=2, dk=hd, dv=2 * hd,
                          hpb_ctx=4, hpb_lat=2, tq_lat=_pick(dseq, 256, GRID_W), name="attn_diff",
                          head_operands=[(lam, (4, hd)), (vec(diff_subln[j]), (1, 2 * hd))], lambda_init=lambda_init)
            w_o = diff_w_o[j]

        else:
            w = na_w_qkv[j].astype(BF16)
            n_w = w.shape[1] // 3
            heads = n_w // hd
            q_scale = hd ** -0.5 * LOG2_E
            q = _matmul(h, w, BF16, w_cols=(0, n_w), out_scale=q_scale)
            k = _matmul(h, w, F32, w_cols=(n_w, n_w))
            v = _matmul(h, w, F32, w_cols=(2 * n_w, n_w))
            outs["na_k"].append(ctx_out(k, (heads, hd)))
            outs["na_v"].append(ctx_out(v, (heads, hd)))
            ck = cache_na_k[:, j].reshape(db * past, n_w)
            cv = cache_na_v[:, j].reshape(db * past, n_w)
            n_rows = dseq // GRID_W
            tq_na = _pick(dseq, 256, GRID_W)
            band_rows, ridx, valid, thresholds = _na_band_plan(n_rows, tq_na)
            bias = _na_bias(na_rpb[j], ridx, valid)

            def case_of(i):
                return sum((i >= th).astype(jnp.int32) for th in thresholds) if thresholds else 0

            ctx_part, lat_parts = kv_parts(k, v, ck, cv, hd, hd)
            o = attention(q, ctx_part, lat_parts, heads=heads, kv_share=1, n_maps=1, dk=hd, dv=hd,
                          hpb_ctx=8, hpb_lat=4, tq_lat=tq_na, name="attn_na", bias=(bias, case_of),
                          band=(n_rows, band_rows))
            w_o = na_w_o[j]

        y = _matmul(o, w_o.astype(BF16), F32)
        x, h = _rows(x, y, mods, tr, post=(vec(norm_mix_post[layer]), seg, 2),
                     pre=(vec(norm_ffn_pre[layer]), seg, 3, 4))

        act = _matmul(h, ffn_w_gu, BF16, layer=layer, swiglu=True, bm=2048, bn=256, single_buffer_a=True)
        y = _matmul(act, ffn_down_bf16, F32, layer=layer, bk=ffn_w_down.shape[1] // 2)
        post = (vec(norm_ffn_post[layer]), seg, 5)
        if layer + 1 < depth:
            x, h = _rows(x, y, mods, tr, post=post,
                         pre=(vec(norm_mix_pre[layer + 1]), seg_of_tile_at(layer + 1), 0, 1))
        else:
            (y_prompt,) = _rows(x, y, mods, tr, post=post, tile0=0, n_tiles=n_ctx_tiles)
            (y_sample,) = _rows(x, y, mods, tr, post=post, tile0=n_ctx_tiles, n_tiles=ts // tr)

    return (y_prompt.reshape(nb, seq, d), y_sample.reshape(db, dseq, d),
            jnp.stack(outs["gqa_k"], axis=1), jnp.stack(outs["gqa_v"], axis=1),
            jnp.stack(outs["mla_ckv"], axis=1), jnp.stack(outs["mla_kpe"], axis=1),
            jnp.stack(outs["diff_k"], axis=1), jnp.stack(outs["diff_v"], axis=1),
            jnp.stack(outs["na_k"], axis=1), jnp.stack(outs["na_v"], axis=1))
```

```python
import functools
import math

import numpy as np
import jax
import jax.numpy as jnp
from jax import lax
from jax.experimental import pallas as pl
from jax.experimental.pallas import tpu as pltpu

GRID_W = 64
N_MIXERS = 4
ROPE_BASE = 10000.0
NORM_EPS = 1e-6
NEG_INF = -1e30
HEAD_DIM = 128
B_ROPE_DIM = 64
WIN_R = 8
WIN_C = 16
LOG2_E = math.log2(math.e)

LANES = 128
MXU_COLS = 256
MOD_ROWS = 16
V7X_VMEM_BYTES = 64 * 1024 * 1024
VMEM_CAP_BYTES = V7X_VMEM_BYTES - 8 * 1024 * 1024

BF16 = jnp.bfloat16
F32 = jnp.float32


def _pick(n, target, mult):
    best = None
    for t in range(mult, min(n, target) + 1, mult):
        if n % t == 0:
            best = t
    return n if best is None else best


def _params(vmem_estimate_bytes, n_grid):
    limit = int(min(VMEM_CAP_BYTES, max(32 * 1024 * 1024, vmem_estimate_bytes)))
    return pltpu.CompilerParams(dimension_semantics=("arbitrary",) * n_grid, vmem_limit_bytes=limit)


def _adaln_kernel(c_ref, w_ref, b_ref, o_ref):
    c = c_ref[...]
    a = (c * jax.nn.sigmoid(c)).astype(BF16)
    o_ref[...] = jnp.dot(a, w_ref[...].astype(BF16), preferred_element_type=F32) + b_ref[...]


def _adaln(cond, ada_w, ada_b):
    depth, d, n = ada_w.shape
    tn = _pick(n, 512, LANES)
    est = 2 * d * tn * 4 + d * tn * 2 + 4 * MOD_ROWS * (d + tn) * 4
    return pl.pallas_call(
        _adaln_kernel,
        grid=(depth, n // tn),
        in_specs=[
            pl.BlockSpec((MOD_ROWS, d), lambda l, j: (0, 0)),
            pl.BlockSpec((None, d, tn), lambda l, j: (l, 0, j)),
            pl.BlockSpec((None, 1, tn), lambda l, j: (l, 0, j)),
        ],
        out_specs=pl.BlockSpec((None, MOD_ROWS, tn), lambda l, j: (l, 0, j)),
        out_shape=jax.ShapeDtypeStruct((depth, MOD_ROWS, n), F32),
        compiler_params=_params(est + (8 << 20), 2),
        name="adaln",
    )(cond, ada_w, ada_b.reshape(depth, 1, n))


def _rms(x, g):
    return x * lax.rsqrt(jnp.mean(x * x, axis=-1, keepdims=True) + NORM_EPS) * g


def _rows_kernel(*refs, has_post, has_pre, split_tiles):
    refs = list(refs)
    x = refs.pop(0)[...]
    if split_tiles is not None:
        x = jnp.where(pl.program_id(0) < split_tiles, x, refs.pop(0)[...])
    if has_post:
        y_ref, gpost_ref, gate_ref = refs[:3]
        refs = refs[3:]
        x = x + gate_ref[0] * _rms(y_ref[...], gpost_ref[...])
    if has_pre:
        gpre_ref, shift_ref, scale_ref = refs[:3]
        refs = refs[3:]
    if has_post:
        refs.pop(0)[...] = x
    if has_pre:
        refs.pop(0)[...] = (_rms(x, gpre_ref[...]) * (1.0 + scale_ref[0]) + shift_ref[0]).astype(BF16)


def _rows(x, y, mods, tr, post=None, pre=None, tile0=0, n_tiles=None):
    xs = x if isinstance(x, tuple) else (x,)
    t, d = sum(a.shape[0] for a in xs), xs[0].shape[1]
    row_spec = pl.BlockSpec((tr, d), lambda i: (i + tile0, 0))
    split_tiles = None
    args, in_specs = [x], [row_spec]
    if isinstance(x, tuple):
        assert tile0 == 0
        split_tiles = xs[0].shape[0] // tr
        args = list(xs)
        in_specs = [pl.BlockSpec((tr, d), lambda i: (jnp.minimum(i, split_tiles - 1), 0)),
                    pl.BlockSpec((tr, d), lambda i: (jnp.maximum(i - split_tiles, 0), 0))]
    n_tiles = t // tr if n_tiles is None else n_tiles
    out_spec = pl.BlockSpec((tr, d), lambda i: (i, 0))
    vec_spec = pl.BlockSpec((1, d), lambda i: (0, 0))

    def mod_spec(seg_of_tile, slot):
        return pl.BlockSpec((1, 1, d), lambda i: (seg_of_tile(i + tile0), 0, slot))

    out_shape, out_specs = [], []
    if post is not None:
        args += [y, post[0], mods]
        in_specs += [row_spec, vec_spec, mod_spec(post[1], post[2])]
        out_shape.append(jax.ShapeDtypeStruct((n_tiles * tr, d), F32))
        out_specs.append(out_spec)
    if pre is not None:
        args += [pre[0], mods, mods]
        in_specs += [vec_spec, mod_spec(pre[1], pre[2]), mod_spec(pre[1], pre[3])]
        out_shape.append(jax.ShapeDtypeStruct((n_tiles * tr, d), BF16))
        out_specs.append(out_spec)
    est = 2 * tr * d * 4 * (len(args) + 2) + (8 << 20)
    out = pl.pallas_call(
        functools.partial(_rows_kernel, has_post=post is not None, has_pre=pre is not None, split_tiles=split_tiles),
        grid=(n_tiles,),
        in_specs=in_specs,
        out_specs=out_specs,
        out_shape=out_shape,
        compiler_params=_params(est, 1),
        name="rows",
    )(*args)
    return out


def _rope_lanes(y, tabs, half):
    if half == LANES // 2:
        return y * tabs[0] + pltpu.roll(y, LANES // 2, 1) * tabs[1]
    return (y * tabs[0] + pltpu.roll(y, LANES - half, 1) * tabs[1] + pltpu.roll(y, half, 1) * tabs[2])


def _mm_kernel(*refs, nk, swiglu, norm_width, rope_half, rope_every, out_scale, sub_n):
    refs = list(refs)
    a_ref = refs.pop(0)
    w_ref = refs.pop(0)
    w2_ref = refs.pop(0) if swiglu else None
    gain_ref = refs.pop(0) if norm_width else None
    n_tab = 0 if not rope_half else (2 if rope_half == LANES // 2 else 3)
    tab_refs = [refs.pop(0) for _ in range(n_tab)]
    o_ref = refs.pop(0)
    acc_refs = refs

    a = a_ref[...].astype(BF16)
    bn = o_ref.shape[-1]

    def epilogue(vals, col0):
        y = vals[0]
        if swiglu:
            y = (y * jax.nn.sigmoid(y)) * vals[1]
        if norm_width or rope_half:
            tabs = [r[...] for r in tab_refs]
            width = norm_width if norm_width else LANES
            for c in range(y.shape[-1] // width):
                sl = slice(col0 + c * width, col0 + (c + 1) * width)
                yc = y[:, c * width:(c + 1) * width]
                if norm_width:
                    yc = _rms(yc, gain_ref[:, sl])
                if rope_half and c % rope_every == rope_every - 1:
                    yc = _rope_lanes(yc, tabs, rope_half)
                if out_scale != 1.0:
                    yc = yc * out_scale
                o_ref[:, sl] = yc.astype(o_ref.dtype)
        else:
            if out_scale != 1.0:
                y = y * out_scale
            o_ref[:, col0:col0 + y.shape[-1]] = y.astype(o_ref.dtype)

    if nk == 1 and sub_n < bn:
        for c0 in range(0, bn, sub_n):
            epilogue([jnp.dot(a, w_ref[:, c0:c0 + sub_n].astype(BF16), preferred_element_type=F32)], c0)
        return

    parts = [jnp.dot(a, w_ref[...].astype(BF16), preferred_element_type=F32)]
    if swiglu:
        parts.append(jnp.dot(a, w2_ref[...].astype(BF16), preferred_element_type=F32))

    if nk == 1:
        epilogue(parts, 0)
    else:
        k = pl.program_id(2)

        @pl.when(k == 0)
        def _():
            for r, p in zip(acc_refs, parts):
                r[...] = p

        @pl.when(k > 0)
        def _():
            for r, p in zip(acc_refs, parts):
                r[...] += p

        @pl.when(k == nk - 1)
        def _():
            epilogue([r[...] for r in acc_refs], 0)


def _matmul(a, w, out_dtype, *, layer=0, w_cols=None, swiglu=False, gain=None, norm_width=0, rope_tabs=None,
            rope_half=0, rope_every=1, out_scale=1.0, bm=1024, bn=None, bk=None, single_buffer_a=False):
    m, kdim = a.shape
    col0, n = (0, w.shape[-1]) if w_cols is None else w_cols
    if swiglu:
        col0, n = 0, w.shape[-1] // 2
    bm = _pick(m, bm, 8)
    if bn is None:
        bn = 512 if (swiglu or bk) else 1024
    bn = _pick(math.gcd(n, col0) if col0 else n, bn, LANES * rope_every)
    if norm_width > LANES:
        bn = norm_width
    bk = kdim if bk is None else _pick(kdim, bk, LANES)
    nk = kdim // bk
    jb = col0 // bn
    sub_n = bn
    if (norm_width or rope_half) and norm_width <= MXU_COLS and nk == 1:
        sub_n = _pick(bn, MXU_COLS, LANES * rope_every)

    def w_spec(col_blk0):
        if w.ndim == 3:
            return pl.BlockSpec((None, bk, bn), lambda i, j, k: (layer, k, j + col_blk0))
        return pl.BlockSpec((bk, bn), lambda i, j, k: (k, j + col_blk0))

    a_buffers = 1 if (single_buffer_a and nk == 1) else 2
    a_mode = dict(pipeline_mode=pl.Buffered(1)) if a_buffers == 1 else {}
    args = [a, w]
    in_specs = [pl.BlockSpec((bm, bk), lambda i, j, k: (i, k), **a_mode), w_spec(jb)]
    if swiglu:
        args.append(w)
        in_specs.append(w_spec(n // bn))
    if norm_width:
        args.append(gain)
        in_specs.append(pl.BlockSpec((1, bn), lambda i, j, k: (0, j)))
    if rope_half:
        for tab in rope_tabs:
            args.append(tab)
            in_specs.append(pl.BlockSpec((bm, LANES), lambda i, j, k: (i, 0)))
    n_w = 2 if swiglu else 1
    scratch = [pltpu.VMEM((bm, bn), F32) for _ in range(n_w)] if nk > 1 else []
    est = (a_buffers * bm * bk * a.dtype.itemsize + (bm * bk * 2 if a.dtype != BF16 else 0)
           + 2 * n_w * bk * bn * w.dtype.itemsize + (n_w * bk * bn * 2 if w.dtype != BF16 else 0)
           + 2 * bm * bn * jnp.dtype(out_dtype).itemsize + (n_w + 2) * bm * bn * 4 + (4 << 20))
    return pl.pallas_call(
        functools.partial(_mm_kernel, nk=nk, swiglu=swiglu, norm_width=norm_width, rope_half=rope_half,
                          rope_every=rope_every, out_scale=out_scale, sub_n=sub_n),
        grid=(m // bm, n // bn, nk),
        in_specs=in_specs,
        out_specs=pl.BlockSpec((bm, bn), lambda i, j, k: (i, j)),
        out_shape=jax.ShapeDtypeStruct((m, n), out_dtype),
        scratch_shapes=scratch,
        compiler_params=_params(est, 3),
        name="matmul",
    )(*args)


def _dot_nt(q, k):
    return lax.dot_general(q, k, (((1,), (1,)), ((), ())), preferred_element_type=F32)


def _softmax_parts(s_parts):
    m = s_parts[0].max(axis=-1, keepdims=True)
    for s in s_parts[1:]:
        m = jnp.maximum(m, s.max(axis=-1, keepdims=True))
    e_parts = [jnp.exp2(s - m) for s in s_parts]
    denom = e_parts[0].sum(axis=-1, keepdims=True)
    for e in e_parts[1:]:
        denom = denom + e.sum(axis=-1, keepdims=True)
    return e_parts, denom


def _fill_kv(part_refs, kbf, vbf, row0, n_rows):
    k_ref, v_ref = part_refs
    kbf[row0:row0 + n_rows, :] = k_ref[...].astype(BF16)
    vbf[row0:row0 + n_rows, :] = v_ref[...].astype(BF16)


def _fill_mla(part_refs, kbf, vbf, row0, n_rows):
    kv_ref, kpe_ref = part_refs
    d = HEAD_DIM
    kpe = kpe_ref[...].astype(BF16)
    for slot in range(kv_ref.shape[1] // (2 * d)):
        kbf[row0:row0 + n_rows, 2 * slot * d:(2 * slot + 1) * d] = kv_ref[:, 2 * slot * d:(2 * slot + 1) * d]
        kbf[row0:row0 + n_rows, (2 * slot + 1) * d:(2 * slot + 2) * d] = kpe
        vbf[row0:row0 + n_rows, slot * d:(slot + 1) * d] = kv_ref[:, (2 * slot + 1) * d:(2 * slot + 2) * d]


def _attn_kernel(*refs, hpb, kv_share, n_maps, dk, dv, n_parts, refs_per_part, fill, band, lambda_init):
    refs = list(refs)
    if n_maps == 2:
        lam_ref, gain_ref = refs.pop(0), refs.pop(0)
    q_ref = refs.pop(0)
    part_refs = [tuple(refs.pop(0) for _ in range(refs_per_part)) for _ in range(n_parts)]
    bias_ref = refs.pop(0) if band else None
    o_ref = refs.pop(0)
    kbf, vbf = refs
    tq = q_ref.shape[0]
    part_rows = [p[0].shape[0] for p in part_refs]

    @pl.when(pl.program_id(2) == 0)
    def _():
        row0 = 0
        for p, n in zip(part_refs, part_rows):
            fill(p, kbf, vbf, row0, n)
            row0 += n

    if band:
        n_rows, band_rows = band
        wr = min(WIN_R, n_rows)
        r0 = pl.program_id(2) * (tq // GRID_W)
        assert part_rows[0] % GRID_W == 0
        start = part_rows[0] + jnp.clip(r0 - wr // 2, 0, n_rows - band_rows) * GRID_W
        key_rows = [pl.ds(0, part_rows[0]), pl.ds(pl.multiple_of(start, GRID_W), band_rows * GRID_W)]
    else:
        key_rows = [pl.ds(0, sum(part_rows))]

    if n_maps == 2:
        lam = (jnp.exp(jnp.sum(lam_ref[0:1, :] * lam_ref[1:2, :], axis=-1, keepdims=True))
               - jnp.exp(jnp.sum(lam_ref[2:3, :] * lam_ref[3:4, :], axis=-1, keepdims=True)) + lambda_init)

    for hs in range(hpb):
        slot = hs // kv_share
        maps = []
        for mp in range(n_maps):
            q = q_ref[:, (hs * n_maps + mp) * dk:(hs * n_maps + mp + 1) * dk]
            kcols = slice((slot * n_maps + mp) * dk, (slot * n_maps + mp + 1) * dk)
            s_parts = [_dot_nt(q, kbf[rows, kcols]) for rows in key_rows]
            if band:
                s_parts[-1] = s_parts[-1] + bias_ref[hs]
            maps.append(_softmax_parts(s_parts))
        vcols = slice(slot * dv, (slot + 1) * dv)
        if n_maps == 1:
            (e_parts, denom), = maps
            wgts = [e.astype(BF16) for e in e_parts]
        else:
            (e0, l0), (e1, l1) = maps
            ratio = lam * l0 / l1
            wgts = [(a - b * ratio).astype(BF16) for a, b in zip(e0, e1)]
            denom = l0
        o = jnp.dot(wgts[0], vbf[key_rows[0], vcols], preferred_element_type=F32)
        for wgt, rows in zip(wgts[1:], key_rows[1:]):
            o = o + jnp.dot(wgt, vbf[rows, vcols], preferred_element_type=F32)
        o = o / denom
        if n_maps == 2:
            o = _rms(o, gain_ref[...]) * (1.0 - lambda_init)
        o_ref[:, hs * dv:(hs + 1) * dv] = o.astype(o_ref.dtype)


def _drop_ref(kernel_fn, index):
    def wrapped(*refs):
        return kernel_fn(*refs[:index], *refs[index + 1:])
    return wrapped


def _attention_call(grid, operands, *, t_rows, heads, tq, q_tile_of, key_rows, cfg, name, into=None):
    hpb, dv, n_maps, dk = cfg["hpb"], cfg["dv"], cfg["n_maps"], cfg["dk"]
    slots = hpb // cfg["kv_share"]
    in_specs = [pl.BlockSpec(blk, imap) for _, blk, imap in operands]
    args = [arr for arr, _, _ in operands]
    kernel_fn = functools.partial(_attn_kernel, **cfg)
    aliases = {}
    if into is not None:
        kernel_fn = _drop_ref(kernel_fn, len(args))
        aliases = {len(args): 0}
        in_specs.append(pl.BlockSpec(memory_space=pl.ANY))
        args.append(into)
    est = (hpb * n_maps * tq * key_rows * 10 + 3 * key_rows * slots * (n_maps * dk + dv) * 4
           + 4 * tq * hpb * (n_maps * dk + dv) * 2 + (8 << 20))
    return pl.pallas_call(
        kernel_fn,
        grid=grid,
        in_specs=in_specs,
        out_specs=pl.BlockSpec((tq, hpb * dv), lambda b, g, i: (q_tile_of(b, i), g)),
        out_shape=jax.ShapeDtypeStruct((t_rows, heads * dv), BF16),
        scratch_shapes=[pltpu.VMEM((key_rows, slots * n_maps * dk), BF16),
                        pltpu.VMEM((key_rows, slots * dv), BF16)],
        input_output_aliases=aliases,
        compiler_params=_params(est, 3),
        name=name,
    )(*args)


def _rope_tables(n_ctx_rows, dec_batch, dec_seq, rot_dim):
    t = jnp.arange(dec_seq)
    per_axis = rot_dim // 2
    inv_freq = ROPE_BASE ** (-jnp.arange(0, per_axis, 2, dtype=F32) / per_axis)
    row = (t // GRID_W).astype(F32)[:, None] * inv_freq
    col = (t % GRID_W).astype(F32)[:, None] * inv_freq
    ang = jnp.concatenate([row, col], axis=-1)
    cos, sin = jnp.cos(ang), jnp.sin(ang)
    pad = LANES - rot_dim
    ones = jnp.ones((dec_seq, pad), F32)
    zeros_half = jnp.zeros_like(sin)
    zeros_pad = jnp.zeros((dec_seq, pad), F32)
    if pad == 0:
        tabs = [jnp.concatenate([cos, cos], -1), jnp.concatenate([-sin, sin], -1)]
    else:
        tabs = [jnp.concatenate([cos, cos, ones], -1),
                jnp.concatenate([-sin, zeros_half, zeros_pad], -1),
                jnp.concatenate([zeros_half, sin, zeros_pad], -1)]
    out = []
    for i, tab in enumerate(tabs):
        ctx = jnp.ones((n_ctx_rows, LANES), F32) if i == 0 else jnp.zeros((n_ctx_rows, LANES), F32)
        out.append(jnp.concatenate([ctx, jnp.tile(tab, (dec_batch, 1))], axis=0))
    return out


def _na_band_plan(n_rows, tq):
    wr = min(WIN_R, n_rows)
    rt = tq // GRID_W
    rows_per_lane_tile = LANES // GRID_W
    band_rows = min(n_rows, -(-(wr + rt - 1) // rows_per_lane_tile) * rows_per_lane_tile)
    cases, case_of_tile = [], []
    for qt in range(n_rows // rt):
        r0 = qt * rt
        bs = int(np.clip(r0 - wr // 2, 0, n_rows - band_rows))
        r = r0 + np.arange(rt)[:, None]
        kr = bs + np.arange(band_rows)[None, :]
        rs = np.clip(r - wr // 2, 0, n_rows - wr)
        valid = (kr >= rs) & (kr < rs + wr)
        ridx = np.clip(kr - r + WIN_R - 1, 0, 2 * WIN_R - 2)
        key = (valid.tobytes(), ridx.tobytes())
        for ci, (k0, _, _) in enumerate(cases):
            if k0 == key:
                break
        else:
            ci = len(cases)
            cases.append((key, ridx, valid))
        case_of_tile.append(ci)
    if any(b < a for a, b in zip(case_of_tile, case_of_tile[1:])) or \
            sorted(set(case_of_tile)) != list(range(len(cases))):
        raise NotImplementedError("neighbourhood tile geometries are not monotone in the tile index")
    thresholds = [case_of_tile.index(ci) for ci in range(1, len(cases))]
    ridx = np.stack([c[1] for c in cases])
    valid = np.stack([c[2] for c in cases])
    return band_rows, ridx, valid, thresholds


def _na_bias(rpb, ridx, valid):
    col = jnp.arange(GRID_W)
    col_start = jnp.clip(col - WIN_C // 2, 0, GRID_W - WIN_C)
    in_win = (col[None, :] >= col_start[:, None]) & (col[None, :] < col_start[:, None] + WIN_C)
    col_off = jnp.clip(col[None, :] - col[:, None] + WIN_C - 1, 0, 2 * WIN_C - 2)
    col_bias = jnp.where(in_win[None, None], rpb[:, :, col_off].astype(F32) * LOG2_E, NEG_INF)
    n_case, rt, band_rows = ridx.shape
    blk = jnp.take(col_bias, jnp.asarray(ridx.reshape(-1)), axis=1)
    blk = blk.reshape(rpb.shape[0], n_case, rt, band_rows, GRID_W, GRID_W)
    blk = jnp.where(jnp.asarray(valid)[None, :, :, :, None, None], blk, NEG_INF)
    blk = jnp.transpose(blk, (0, 1, 2, 4, 3, 5))
    return blk.reshape(rpb.shape[0], n_case, rt * GRID_W, band_rows * GRID_W)


def kernel(x_prompt, x_sample, cache_gqa_k, cache_gqa_v, cache_mla_ckv, cache_mla_kpe, cache_diff_k, cache_diff_v, cache_na_k, cache_na_v, c, c_ctx, ada_w, ada_b, norm_mix_pre, norm_mix_post, norm_ffn_pre, norm_ffn_post, ffn_w_gu, ffn_w_down, gqa_w_qkv, gqa_q_norm, gqa_k_norm, gqa_w_o, mla_w_in, mla_q_norm, mla_kv_norm, mla_w_q_up, mla_w_kv_up, mla_w_o, diff_w_qkv, diff_lam_q1, diff_lam_k1, diff_lam_q2, diff_lam_k2, diff_subln, diff_w_o, na_w_qkv, na_rpb, na_w_o):
    nb, seq, d = x_prompt.shape
    db, dseq, _ = x_sample.shape
    past = cache_gqa_k.shape[2]
    depth = ada_w.shape[0]
    tp, ts = nb * seq, db * dseq
    t = tp + ts
    hd = HEAD_DIM
    assert db + 1 <= MOD_ROWS and dseq % GRID_W == 0
    assert tp % dseq == 0 and t % past == 0 and tp % past == 0

    tr = _pick(math.gcd(seq, dseq), 256, 8)
    n_ctx_tiles, tiles_per_dec = tp // tr, dseq // tr

    def seg_of_tile_at(layer):
        return lambda i: layer * MOD_ROWS + jnp.where(i < n_ctx_tiles, 0, 1 + (i - n_ctx_tiles) // tiles_per_dec)

    cond = jnp.concatenate([c_ctx[None, :], c, jnp.zeros((MOD_ROWS - 1 - db, d), F32)], axis=0)
    mods = _adaln(cond, ada_w, ada_b).reshape(depth * MOD_ROWS, 1, 6 * d)

    x = (x_prompt.reshape(tp, d), x_sample.reshape(ts, d))
    o_buf = [None]

    rope128 = _rope_tables(tp, db, dseq, hd)
    rope64 = _rope_tables(tp, db, dseq, B_ROPE_DIM)
    lat_blk0 = tp // dseq
    tq_dense = _pick(dseq, 512, GRID_W)

    def vec(v):
        return v.reshape(1, -1).astype(F32)

    def ctx_out(arr, shape):
        return arr[:tp].reshape((nb, seq) + shape)

    def attention(q, ctx_part, lat_parts, *, heads, kv_share, n_maps, dk, dv, hpb_ctx, hpb_lat, tq_lat, name,
                  fill=_fill_kv, head_operands=(), bias=None, band=None, lambda_init=0.0):
        def part_operands(part, hpb):
            slots = hpb // kv_share
            ops = []
            for arr, rows, blk_of, cols in part:
                if cols is None:
                    ops.append((arr, (rows, LANES), lambda b, g, i, blk_of=blk_of: (blk_of(b), 0)))
                else:
                    ops.append((arr, (rows, slots * cols), lambda b, g, i, blk_of=blk_of: (blk_of(b), g)))
            return ops

        def cfg(hpb, n_parts, band_cfg):
            return dict(hpb=hpb, kv_share=kv_share, n_maps=n_maps, dk=dk, dv=dv, n_parts=n_parts,
                        refs_per_part=len(ctx_part), fill=fill, band=band_cfg, lambda_init=lambda_init)

        qw = n_maps * dk
        hpb_c = min(hpb_ctx, heads)
        ops = [(a, blk, (lambda b, g, i: (0, 0))) for a, blk in head_operands]
        ops.append((q, (seq, hpb_c * qw), lambda b, g, i: (b, g)))
        ops += part_operands(ctx_part, hpb_c)
        buf = o_buf[0]
        if buf is None or buf.shape != (t, heads * dv):
            buf = jnp.zeros((t, heads * dv), BF16)
        o = _attention_call((nb, heads // hpb_c, 1), ops, t_rows=t, heads=heads, tq=seq,
                            q_tile_of=lambda b, i: b, key_rows=seq, cfg=cfg(hpb_c, 1, None), name=name + "_ctx",
                            into=buf)
        hpb_l = min(hpb_lat, heads)
        n_qt, ctx_tiles = dseq // tq_lat, tp // tq_lat

        def q_tile_of(b, i):
            return ctx_tiles + b * n_qt + i

        ops = [(a, blk, (lambda b, g, i: (0, 0))) for a, blk in head_operands]
        ops.append((q, (tq_lat, hpb_l * qw), lambda b, g, i: (q_tile_of(b, i), g)))
        for part in lat_parts:
            ops += part_operands(part, hpb_l)
        if bias is not None:
            bias_arr, case_of = bias
            ops.append((bias_arr, (hpb_l, None) + bias_arr.shape[2:], lambda b, g, i: (g, case_of(i), 0, 0)))
        return _attention_call((db, heads // hpb_l, n_qt), ops, t_rows=t, heads=heads, tq=tq_lat,
                               q_tile_of=q_tile_of, key_rows=past + dseq, cfg=cfg(hpb_l, len(lat_parts), band),
                               name=name + "_lat", into=o)

    def kv_parts(k, v, ck, cv, cols_k, cols_v):
        ctx = [(k, seq, lambda b: b, cols_k), (v, seq, lambda b: b, cols_v)]
        lat = [[(ck, past, lambda b: b, cols_k), (cv, past, lambda b: b, cols_v)],
               [(k, dseq, lambda b: lat_blk0 + b, cols_k), (v, dseq, lambda b: lat_blk0 + b, cols_v)]]
        return ctx, lat

    outs = {k: [] for k in ("gqa_k", "gqa_v", "mla_ckv", "mla_kpe", "diff_k", "diff_v", "na_k", "na_v")}

    ffn_down_bf16 = ffn_w_down.astype(BF16)

    (h,) = _rows(x, None, mods, tr, pre=(vec(norm_mix_pre[0]), seg_of_tile_at(0), 0, 1))

    for layer in range(depth):
        kind, j = layer % N_MIXERS, layer // N_MIXERS
        seg = seg_of_tile_at(layer)

        if kind == 0:
            w = gqa_w_qkv[j].astype(BF16)
            n_q = gqa_w_o.shape[1]
            n_kv = (w.shape[1] - n_q) // 2
            heads, kv_heads = n_q // hd, n_kv // hd
            group = heads // kv_heads
            q_scale = hd ** -0.5 * LOG2_E
            q = _matmul(h, w, BF16, w_cols=(0, n_q), gain=jnp.tile(vec(gqa_q_norm[j]), (1, heads)), norm_width=hd,
                        rope_tabs=rope128, rope_half=hd // 2, out_scale=q_scale)
            k = _matmul(h, w, F32, w_cols=(n_q, n_kv), gain=jnp.tile(vec(gqa_k_norm[j]), (1, kv_heads)),
                        norm_width=hd, rope_tabs=rope128, rope_half=hd // 2)
            v = _matmul(h, w, F32, w_cols=(n_q + n_kv, n_kv))
            outs["gqa_k"].append(ctx_out(k, (kv_heads, hd)))
            outs["gqa_v"].append(ctx_out(v, (kv_heads, hd)))
            ck = cache_gqa_k[:, j].reshape(db * past, n_kv)
            cv = cache_gqa_v[:, j].reshape(db * past, n_kv)
            ctx_part, lat_parts = kv_parts(k, v, ck, cv, hd, hd)
            o = attention(q, ctx_part, lat_parts, heads=heads, kv_share=group, n_maps=1, dk=hd, dv=hd,
                          hpb_ctx=2 * group, hpb_lat=group, tq_lat=tq_dense, name="attn_gqa")
            w_o = gqa_w_o[j]

        elif kind == 1:
            q_rank, kv_rank = mla_q_norm.shape[1], mla_kv_norm.shape[1]
            w_in = mla_w_in[j].astype(BF16)
            rope_dim = w_in.shape[1] - q_rank - kv_rank
            w_kpe = jnp.pad(w_in[:, q_rank + kv_rank:], ((0, 0), (0, LANES - rope_dim)))
            heads = mla_w_o.shape[1] // hd
            q_scale = (hd + rope_dim) ** -0.5 * LOG2_E
            cq = _matmul(h, w_in, BF16, w_cols=(0, q_rank), gain=vec(mla_q_norm[j]), norm_width=q_rank)
            ckv = _matmul(h, w_in, F32, w_cols=(q_rank, kv_rank), gain=vec(mla_kv_norm[j]), norm_width=kv_rank)
            kpe = _matmul(h, w_kpe, F32, rope_tabs=rope64, rope_half=rope_dim // 2)
            outs["mla_ckv"].append(ctx_out(ckv, (kv_rank,)))
            outs["mla_kpe"].append(ctx_out(kpe[:, :rope_dim], (rope_dim,)))
            w_q = jnp.pad(mla_w_q_up[j].astype(BF16).reshape(q_rank, heads, hd + rope_dim),
                          ((0, 0), (0, 0), (0, LANES - rope_dim))).reshape(q_rank, heads * 2 * hd)
            q = _matmul(cq, w_q, BF16, rope_tabs=rope64, rope_half=rope_dim // 2, rope_every=2, out_scale=q_scale)
            ckv_all = jnp.concatenate([ckv, cache_mla_ckv[:, j].reshape(db * past, kv_rank)], axis=0)
            kv = _matmul(ckv_all, mla_w_kv_up[j].astype(BF16), BF16)
            ckpe = jnp.pad(cache_mla_kpe[:, j].reshape(db * past, rope_dim), ((0, 0), (0, LANES - rope_dim)))
            cache_blk0 = t // past
            ctx_part = [(kv, seq, lambda b: b, 2 * hd), (kpe, seq, lambda b: b, None)]
            lat_parts = [[(kv, past, lambda b: cache_blk0 + b, 2 * hd), (ckpe, past, lambda b: b, None)],
                         [(kv, dseq, lambda b: lat_blk0 + b, 2 * hd), (kpe, dseq, lambda b: lat_blk0 + b, None)]]
            o = attention(q, ctx_part, lat_parts, heads=heads, kv_share=1, n_maps=1, dk=2 * hd, dv=hd,
                          hpb_ctx=8, hpb_lat=2, tq_lat=tq_dense, name="attn_mla", fill=_fill_mla)
            w_o = mla_w_o[j]

        elif kind == 2:
            lambda_init = 0.8 - 0.6 * math.exp(-0.3 * layer)
            w = diff_w_qkv[j].astype(BF16)
            n_w = w.shape[1] // 3
            heads = n_w // (2 * hd)
            q_scale = hd ** -0.5 * LOG2_E
            q = _matmul(h, w, BF16, w_cols=(0, n_w), rope_tabs=rope128, rope_half=hd // 2, out_scale=q_scale)
            k = _matmul(h, w, F32, w_cols=(n_w, n_w), rope_tabs=rope128, rope_half=hd // 2)
            v = _matmul(h, w, F32, w_cols=(2 * n_w, n_w))
            outs["diff_k"].append(ctx_out(k, (heads, 2, hd)))
            outs["diff_v"].append(ctx_out(v, (heads, 2 * hd)))
            ck = cache_diff_k[:, j].reshape(db * past, n_w)
            cv = cache_diff_v[:, j].reshape(db * past, n_w)
            lam = jnp.stack([diff_lam_q1[j], diff_lam_k1[j], diff_lam_q2[j], diff_lam_k2[j]]).astype(F32)
            ctx_part, lat_parts = kv_parts(k, v, ck, cv, 2 * hd, 2 * hd)
            o = attention(q, ctx_part, lat_parts, heads=heads, kv_share=1, n_maps=2, dk=hd, dv=2 * hd,
                          hpb_ctx=4, hpb_lat=2, tq_lat=tq_dense, name="attn_diff",
                          head_operands=[(lam, (4, hd)), (vec(diff_subln[j]), (1, 2 * hd))], lambda_init=lambda_init)
            w_o = diff_w_o[j]

        else:
            w = na_w_qkv[j].astype(BF16)
            n_w = w.shape[1] // 3
            heads = n_w // hd
            q_scale = hd ** -0.5 * LOG2_E
            q = _matmul(h, w, BF16, w_cols=(0, n_w), out_scale=q_scale)
            k = _matmul(h, w, F32, w_cols=(n_w, n_w))
            v = _matmul(h, w, F32, w_cols=(2 * n_w, n_w))
            outs["na_k"].append(ctx_out(k, (heads, hd)))
            outs["na_v"].append(ctx_out(v, (heads, hd)))
            ck = cache_na_k[:, j].reshape(db * past, n_w)
            cv = cache_na_v[:, j].reshape(db * past, n_w)
            n_rows = dseq // GRID_W
            tq_na = _pick(dseq, 256, GRID_W)
            band_rows, ridx, valid, thresholds = _na_band_plan(n_rows, tq_na)
            bias = _na_bias(na_rpb[j], ridx, valid)

            def case_of(i):
                return sum((i >= th).astype(jnp.int32) for th in thresholds) if thresholds else 0

            ctx_part, lat_parts = kv_parts(k, v, ck, cv, hd, hd)
            o = attention(q, ctx_part, lat_parts, heads=heads, kv_share=1, n_maps=1, dk=hd, dv=hd,
                          hpb_ctx=8, hpb_lat=4, tq_lat=tq_na, name="attn_na", bias=(bias, case_of),
                          band=(n_rows, band_rows))
            w_o = na_w_o[j]

        y = _matmul(o, w_o.astype(BF16), F32)
        o_buf[0] = o
        x, h = _rows(x, y, mods, tr, post=(vec(norm_mix_post[layer]), seg, 2),
                     pre=(vec(norm_ffn_pre[layer]), seg, 3, 4))

        act = _matmul(h, ffn_w_gu, BF16, layer=layer, swiglu=True, bm=2048, bn=256, single_buffer_a=True)
        y = _matmul(act, ffn_down_bf16, F32, layer=layer, bk=ffn_w_down.shape[1] // 2)
        post = (vec(norm_ffn_post[layer]), seg, 5)
        if layer + 1 < depth:
            x, h = _rows(x, y, mods, tr, post=post,
                         pre=(vec(norm_mix_pre[layer + 1]), seg_of_tile_at(layer + 1), 0, 1))
        else:
            (y_prompt,) = _rows(x, y, mods, tr, post=post, tile0=0, n_tiles=n_ctx_tiles)
            (y_sample,) = _rows(x, y, mods, tr, post=post, tile0=n_ctx_tiles, n_tiles=ts // tr)

    return (y_prompt.reshape(nb, seq, d), y_sample.reshape(db, dseq, d),
            jnp.stack(outs["gqa_k"], axis=1), jnp.stack(outs["gqa_v"], axis=1),
            jnp.stack(outs["mla_ckv"], axis=1), jnp.stack(outs["mla_kpe"], axis=1),
            jnp.stack(outs["diff_k"], axis=1), jnp.stack(outs["diff_v"], axis=1),
            jnp.stack(outs["na_k"], axis=1), jnp.stack(outs["na_v"], axis=1))
```

```python
import functools
import math

import numpy as np
import jax
import jax.numpy as jnp
from jax import lax
from jax.experimental import pallas as pl
from jax.experimental.pallas import tpu as pltpu

GRID_W = 64
N_MIXERS = 4
ROPE_BASE = 10000.0
NORM_EPS = 1e-6
NEG_INF = -1e30
HEAD_DIM = 128
B_ROPE_DIM = 64
WIN_R = 8
WIN_C = 16
LOG2_E = math.log2(math.e)

LANES = 128
MXU_COLS = 256
MOD_ROWS = 16
V7X_VMEM_BYTES = 64 * 1024 * 1024
VMEM_CAP_BYTES = V7X_VMEM_BYTES - 8 * 1024 * 1024

BF16 = jnp.bfloat16
F32 = jnp.float32


def _pick(n, target, mult):
    best = None
    for t in range(mult, min(n, target) + 1, mult):
        if n % t == 0:
            best = t
    return n if best is None else best


def _params(vmem_estimate_bytes, n_grid):
    limit = int(min(VMEM_CAP_BYTES, max(32 * 1024 * 1024, vmem_estimate_bytes)))
    return pltpu.CompilerParams(dimension_semantics=("arbitrary",) * n_grid, vmem_limit_bytes=limit)


def _adaln_kernel(c_ref, w_ref, b_ref, o_ref):
    c = c_ref[...]
    a = (c * jax.nn.sigmoid(c)).astype(BF16)
    o_ref[...] = jnp.dot(a, w_ref[...].astype(BF16), preferred_element_type=F32) + b_ref[...]


def _adaln(cond, ada_w, ada_b):
    depth, d, n = ada_w.shape
    tn = _pick(n, 512, LANES)
    est = 2 * d * tn * 4 + d * tn * 2 + 4 * MOD_ROWS * (d + tn) * 4
    return pl.pallas_call(
        _adaln_kernel,
        grid=(depth, n // tn),
        in_specs=[
            pl.BlockSpec((MOD_ROWS, d), lambda l, j: (0, 0)),
            pl.BlockSpec((None, d, tn), lambda l, j: (l, 0, j)),
            pl.BlockSpec((None, 1, tn), lambda l, j: (l, 0, j)),
        ],
        out_specs=pl.BlockSpec((None, MOD_ROWS, tn), lambda l, j: (l, 0, j)),
        out_shape=jax.ShapeDtypeStruct((depth, MOD_ROWS, n), F32),
        compiler_params=_params(est + (8 << 20), 2),
        name="adaln",
    )(cond, ada_w, ada_b.reshape(depth, 1, n))


def _rms(x, g):
    return x * lax.rsqrt(jnp.mean(x * x, axis=-1, keepdims=True) + NORM_EPS) * g


def _rows_kernel(*refs, has_post, has_pre, split_tiles):
    refs = list(refs)
    x = refs.pop(0)[...]
    if split_tiles is not None:
        x = jnp.where(pl.program_id(0) < split_tiles, x, refs.pop(0)[...])
    if has_post:
        y_ref, gpost_ref, gate_ref = refs[:3]
        refs = refs[3:]
        x = x + gate_ref[0] * _rms(y_ref[...], gpost_ref[...])
    if has_pre:
        gpre_ref, shift_ref, scale_ref = refs[:3]
        refs = refs[3:]
    if has_post:
        refs.pop(0)[...] = x
    if has_pre:
        refs.pop(0)[...] = (_rms(x, gpre_ref[...]) * (1.0 + scale_ref[0]) + shift_ref[0]).astype(BF16)


def _rows(x, y, mods, tr, post=None, pre=None, tile0=0, n_tiles=None):
    xs = x if isinstance(x, tuple) else (x,)
    t, d = sum(a.shape[0] for a in xs), xs[0].shape[1]
    row_spec = pl.BlockSpec((tr, d), lambda i: (i + tile0, 0))
    split_tiles = None
    args, in_specs = [x], [row_spec]
    if isinstance(x, tuple):
        assert tile0 == 0
        split_tiles = xs[0].shape[0] // tr
        args = list(xs)
        in_specs = [pl.BlockSpec((tr, d), lambda i: (jnp.minimum(i, split_tiles - 1), 0)),
                    pl.BlockSpec((tr, d), lambda i: (jnp.maximum(i - split_tiles, 0), 0))]
    n_tiles = t // tr if n_tiles is None else n_tiles
    out_spec = pl.BlockSpec((tr, d), lambda i: (i, 0))
    vec_spec = pl.BlockSpec((1, d), lambda i: (0, 0))

    def mod_spec(seg_of_tile, slot):
        return pl.BlockSpec((1, 1, d), lambda i: (seg_of_tile(i + tile0), 0, slot))

    out_shape, out_specs = [], []
    if post is not None:
        args += [y, post[0], mods]
        in_specs += [row_spec, vec_spec, mod_spec(post[1], post[2])]
        out_shape.append(jax.ShapeDtypeStruct((n_tiles * tr, d), F32))
        out_specs.append(out_spec)
    if pre is not None:
        args += [pre[0], mods, mods]
        in_specs += [vec_spec, mod_spec(pre[1], pre[2]), mod_spec(pre[1], pre[3])]
        out_shape.append(jax.ShapeDtypeStruct((n_tiles * tr, d), BF16))
        out_specs.append(out_spec)
    est = 2 * tr * d * 4 * (len(args) + 2) + (8 << 20)
    out = pl.pallas_call(
        functools.partial(_rows_kernel, has_post=post is not None, has_pre=pre is not None, split_tiles=split_tiles),
        grid=(n_tiles,),
        in_specs=in_specs,
        out_specs=out_specs,
        out_shape=out_shape,
        compiler_params=_params(est, 1),
        name="rows",
    )(*args)
    return out


def _rope_lanes(y, tabs, half):
    if half == LANES // 2:
        return y * tabs[0] + pltpu.roll(y, LANES // 2, 1) * tabs[1]
    return (y * tabs[0] + pltpu.roll(y, LANES - half, 1) * tabs[1] + pltpu.roll(y, half, 1) * tabs[2])


def _mm_kernel(*refs, nk, swiglu, norm_width, rope_half, rope_every, out_scale, sub_n):
    refs = list(refs)
    a_ref = refs.pop(0)
    w_ref = refs.pop(0)
    w2_ref = refs.pop(0) if swiglu else None
    gain_ref = refs.pop(0) if norm_width else None
    n_tab = 0 if not rope_half else (2 if rope_half == LANES // 2 else 3)
    tab_refs = [refs.pop(0) for _ in range(n_tab)]
    o_ref = refs.pop(0)
    acc_refs = refs

    a = a_ref[...].astype(BF16)
    bn = o_ref.shape[-1]

    def epilogue(vals, col0):
        y = vals[0]
        if swiglu:
            y = (y * jax.nn.sigmoid(y)) * vals[1]
        if norm_width or rope_half:
            tabs = [r[...] for r in tab_refs]
            width = norm_width if norm_width else LANES
            for c in range(y.shape[-1] // width):
                sl = slice(col0 + c * width, col0 + (c + 1) * width)
                yc = y[:, c * width:(c + 1) * width]
                if norm_width:
                    yc = _rms(yc, gain_ref[:, sl])
                if rope_half and c % rope_every == rope_every - 1:
                    yc = _rope_lanes(yc, tabs, rope_half)
                if out_scale != 1.0:
                    yc = yc * out_scale
                o_ref[:, sl] = yc.astype(o_ref.dtype)
        else:
            if out_scale != 1.0:
                y = y * out_scale
            o_ref[:, col0:col0 + y.shape[-1]] = y.astype(o_ref.dtype)

    if nk == 1 and sub_n < bn:
        for c0 in range(0, bn, sub_n):
            epilogue([jnp.dot(a, w_ref[:, c0:c0 + sub_n].astype(BF16), preferred_element_type=F32)], c0)
        return

    parts = [jnp.dot(a, w_ref[...].astype(BF16), preferred_element_type=F32)]
    if swiglu:
        parts.append(jnp.dot(a, w2_ref[...].astype(BF16), preferred_element_type=F32))

    if nk == 1:
        epilogue(parts, 0)
    else:
        k = pl.program_id(2)

        @pl.when(k == 0)
        def _():
            for r, p in zip(acc_refs, parts):
                r[...] = p

        @pl.when(k > 0)
        def _():
            for r, p in zip(acc_refs, parts):
                r[...] += p

        @pl.when(k == nk - 1)
        def _():
            epilogue([r[...] for r in acc_refs], 0)


def _matmul(a, w, out_dtype, *, layer=0, w_cols=None, swiglu=False, gain=None, norm_width=0, rope_tabs=None,
            rope_half=0, rope_every=1, out_scale=1.0, bm=1024, bn=None, bk=None, single_buffer_a=False):
    m, kdim = a.shape
    col0, n = (0, w.shape[-1]) if w_cols is None else w_cols
    if swiglu:
        col0, n = 0, w.shape[-1] // 2
    bm = _pick(m, bm, 8)
    if bn is None:
        bn = 512 if (swiglu or bk) else 1024
    bn = _pick(math.gcd(n, col0) if col0 else n, bn, LANES * rope_every)
    if norm_width > LANES:
        bn = norm_width
    bk = kdim if bk is None else _pick(kdim, bk, LANES)
    nk = kdim // bk
    jb = col0 // bn
    sub_n = bn
    if (norm_width or rope_half) and norm_width <= MXU_COLS and nk == 1:
        sub_n = _pick(bn, MXU_COLS, LANES * rope_every)

    def w_spec(col_blk0):
        if w.ndim == 3:
            return pl.BlockSpec((None, bk, bn), lambda i, j, k: (layer, k, j + col_blk0))
        return pl.BlockSpec((bk, bn), lambda i, j, k: (k, j + col_blk0))

    a_buffers = 1 if (single_buffer_a and nk == 1) else 2
    a_mode = dict(pipeline_mode=pl.Buffered(1)) if a_buffers == 1 else {}
    args = [a, w]
    in_specs = [pl.BlockSpec((bm, bk), lambda i, j, k: (i, k), **a_mode), w_spec(jb)]
    if swiglu:
        args.append(w)
        in_specs.append(w_spec(n // bn))
    if norm_width:
        args.append(gain)
        in_specs.append(pl.BlockSpec((1, bn), lambda i, j, k: (0, j)))
    if rope_half:
        for tab in rope_tabs:
            args.append(tab)
            in_specs.append(pl.BlockSpec((bm, LANES), lambda i, j, k: (i, 0)))
    n_w = 2 if swiglu else 1
    scratch = [pltpu.VMEM((bm, bn), F32) for _ in range(n_w)] if nk > 1 else []
    est = (a_buffers * bm * bk * a.dtype.itemsize + (bm * bk * 2 if a.dtype != BF16 else 0)
           + 2 * n_w * bk * bn * w.dtype.itemsize + (n_w * bk * bn * 2 if w.dtype != BF16 else 0)
           + 2 * bm * bn * jnp.dtype(out_dtype).itemsize + (n_w + 2) * bm * bn * 4 + (4 << 20))
    return pl.pallas_call(
        functools.partial(_mm_kernel, nk=nk, swiglu=swiglu, norm_width=norm_width, rope_half=rope_half,
                          rope_every=rope_every, out_scale=out_scale, sub_n=sub_n),
        grid=(m // bm, n // bn, nk),
        in_specs=in_specs,
        out_specs=pl.BlockSpec((bm, bn), lambda i, j, k: (i, j)),
        out_shape=jax.ShapeDtypeStruct((m, n), out_dtype),
        scratch_shapes=scratch,
        compiler_params=_params(est, 3),
        name="matmul",
    )(*args)


def _dot_nt(q, k):
    return lax.dot_general(q, k, (((1,), (1,)), ((), ())), preferred_element_type=F32)


def _softmax_parts(s_parts):
    m = s_parts[0].max(axis=-1, keepdims=True)
    for s in s_parts[1:]:
        m = jnp.maximum(m, s.max(axis=-1, keepdims=True))
    e_parts = [jnp.exp2(s - m) for s in s_parts]
    denom = e_parts[0].sum(axis=-1, keepdims=True)
    for e in e_parts[1:]:
        denom = denom + e.sum(axis=-1, keepdims=True)
    return e_parts, denom


def _fill_kv(part_refs, kbf, vbf, row0, n_rows):
    k_ref, v_ref = part_refs
    kbf[row0:row0 + n_rows, :] = k_ref[...].astype(BF16)
    vbf[row0:row0 + n_rows, :] = v_ref[...].astype(BF16)


def _fill_mla(part_refs, kbf, vbf, row0, n_rows):
    kv_ref, kpe_ref = part_refs
    d = HEAD_DIM
    kpe = kpe_ref[...].astype(BF16)
    for slot in range(kv_ref.shape[1] // (2 * d)):
        kbf[row0:row0 + n_rows, 2 * slot * d:(2 * slot + 1) * d] = kv_ref[:, 2 * slot * d:(2 * slot + 1) * d]
        kbf[row0:row0 + n_rows, (2 * slot + 1) * d:(2 * slot + 2) * d] = kpe
        vbf[row0:row0 + n_rows, slot * d:(slot + 1) * d] = kv_ref[:, (2 * slot + 1) * d:(2 * slot + 2) * d]


def _attn_kernel(*refs, hpb, kv_share, n_maps, dk, dv, n_parts, refs_per_part, fill, band, lambda_init):
    refs = list(refs)
    if n_maps == 2:
        lam_ref, gain_ref = refs.pop(0), refs.pop(0)
    q_ref = refs.pop(0)
    part_refs = [tuple(refs.pop(0) for _ in range(refs_per_part)) for _ in range(n_parts)]
    bias_ref = refs.pop(0) if band else None
    o_ref = refs.pop(0)
    kbf, vbf = refs
    tq = q_ref.shape[0]
    part_rows = [p[0].shape[0] for p in part_refs]

    @pl.when(pl.program_id(2) == 0)
    def _():
        row0 = 0
        for p, n in zip(part_refs, part_rows):
            fill(p, kbf, vbf, row0, n)
            row0 += n

    if band:
        n_rows, band_rows = band
        wr = min(WIN_R, n_rows)
        r0 = pl.program_id(2) * (tq // GRID_W)
        assert part_rows[0] % GRID_W == 0
        start = part_rows[0] + jnp.clip(r0 - wr // 2, 0, n_rows - band_rows) * GRID_W
        key_rows = [pl.ds(0, part_rows[0]), pl.ds(pl.multiple_of(start, GRID_W), band_rows * GRID_W)]
    else:
        key_rows = [pl.ds(0, sum(part_rows))]

    if n_maps == 2:
        lam = (jnp.exp(jnp.sum(lam_ref[0:1, :] * lam_ref[1:2, :], axis=-1, keepdims=True))
               - jnp.exp(jnp.sum(lam_ref[2:3, :] * lam_ref[3:4, :], axis=-1, keepdims=True)) + lambda_init)

    for hs in range(hpb):
        slot = hs // kv_share
        maps = []
        for mp in range(n_maps):
            q = q_ref[:, (hs * n_maps + mp) * dk:(hs * n_maps + mp + 1) * dk]
            kcols = slice((slot * n_maps + mp) * dk, (slot * n_maps + mp + 1) * dk)
            s_parts = [_dot_nt(q, kbf[rows, kcols]) for rows in key_rows]
            if band:
                s_parts[-1] = s_parts[-1] + bias_ref[hs]
            maps.append(_softmax_parts(s_parts))
        vcols = slice(slot * dv, (slot + 1) * dv)
        if n_maps == 1:
            (e_parts, denom), = maps
            wgts = [e.astype(BF16) for e in e_parts]
        else:
            (e0, l0), (e1, l1) = maps
            ratio = lam * l0 / l1
            wgts = [(a - b * ratio).astype(BF16) for a, b in zip(e0, e1)]
            denom = l0
        o = jnp.dot(wgts[0], vbf[key_rows[0], vcols], preferred_element_type=F32)
        for wgt, rows in zip(wgts[1:], key_rows[1:]):
            o = o + jnp.dot(wgt, vbf[rows, vcols], preferred_element_type=F32)
        o = o / denom
        if n_maps == 2:
            o = _rms(o, gain_ref[...]) * (1.0 - lambda_init)
        o_ref[:, hs * dv:(hs + 1) * dv] = o.astype(o_ref.dtype)


def _drop_ref(kernel_fn, index):
    def wrapped(*refs):
        return kernel_fn(*refs[:index], *refs[index + 1:])
    return wrapped


def _attention_call(grid, operands, *, t_rows, heads, tq, q_tile_of, key_rows, cfg, name, into=None):
    hpb, dv, n_maps, dk = cfg["hpb"], cfg["dv"], cfg["n_maps"], cfg["dk"]
    slots = hpb // cfg["kv_share"]
    in_specs = [pl.BlockSpec(blk, imap) for _, blk, imap in operands]
    args = [arr for arr, _, _ in operands]
    kernel_fn = functools.partial(_attn_kernel, **cfg)
    aliases = {}
    if into is not None:
        kernel_fn = _drop_ref(kernel_fn, len(args))
        aliases = {len(args): 0}
        in_specs.append(pl.BlockSpec(memory_space=pl.ANY))
        args.append(into)
    est = (hpb * n_maps * tq * key_rows * 10 + 3 * key_rows * slots * (n_maps * dk + dv) * 4
           + 4 * tq * hpb * (n_maps * dk + dv) * 2 + (8 << 20))
    return pl.pallas_call(
        kernel_fn,
        grid=grid,
        in_specs=in_specs,
        out_specs=pl.BlockSpec((tq, hpb * dv), lambda b, g, i: (q_tile_of(b, i), g)),
        out_shape=jax.ShapeDtypeStruct((t_rows, heads * dv), BF16),
        scratch_shapes=[pltpu.VMEM((key_rows, slots * n_maps * dk), BF16),
                        pltpu.VMEM((key_rows, slots * dv), BF16)],
        input_output_aliases=aliases,
        compiler_params=_params(est, 3),
        name=name,
    )(*args)


def _rope_tables(n_ctx_rows, dec_batch, dec_seq, rot_dim):
    t = jnp.arange(dec_seq)
    per_axis = rot_dim // 2
    inv_freq = ROPE_BASE ** (-jnp.arange(0, per_axis, 2, dtype=F32) / per_axis)
    row = (t // GRID_W).astype(F32)[:, None] * inv_freq
    col = (t % GRID_W).astype(F32)[:, None] * inv_freq
    ang = jnp.concatenate([row, col], axis=-1)
    cos, sin = jnp.cos(ang), jnp.sin(ang)
    pad = LANES - rot_dim
    ones = jnp.ones((dec_seq, pad), F32)
    zeros_half = jnp.zeros_like(sin)
    zeros_pad = jnp.zeros((dec_seq, pad), F32)
    if pad == 0:
        tabs = [jnp.concatenate([cos, cos], -1), jnp.concatenate([-sin, sin], -1)]
    else:
        tabs = [jnp.concatenate([cos, cos, ones], -1),
                jnp.concatenate([-sin, zeros_half, zeros_pad], -1),
                jnp.concatenate([zeros_half, sin, zeros_pad], -1)]
    out = []
    for i, tab in enumerate(tabs):
        ctx = jnp.ones((n_ctx_rows, LANES), F32) if i == 0 else jnp.zeros((n_ctx_rows, LANES), F32)
        out.append(jnp.concatenate([ctx, jnp.tile(tab, (dec_batch, 1))], axis=0))
    return out


def _na_band_plan(n_rows, tq):
    wr = min(WIN_R, n_rows)
    rt = tq // GRID_W
    rows_per_lane_tile = LANES // GRID_W
    band_rows = min(n_rows, -(-(wr + rt - 1) // rows_per_lane_tile) * rows_per_lane_tile)
    cases, case_of_tile = [], []
    for qt in range(n_rows // rt):
        r0 = qt * rt
        bs = int(np.clip(r0 - wr // 2, 0, n_rows - band_rows))
        r = r0 + np.arange(rt)[:, None]
        kr = bs + np.arange(band_rows)[None, :]
        rs = np.clip(r - wr // 2, 0, n_rows - wr)
        valid = (kr >= rs) & (kr < rs + wr)
        ridx = np.clip(kr - r + WIN_R - 1, 0, 2 * WIN_R - 2)
        key = (valid.tobytes(), ridx.tobytes())
        for ci, (k0, _, _) in enumerate(cases):
            if k0 == key:
                break
        else:
            ci = len(cases)
            cases.append((key, ridx, valid))
        case_of_tile.append(ci)
    if any(b < a for a, b in zip(case_of_tile, case_of_tile[1:])) or \
            sorted(set(case_of_tile)) != list(range(len(cases))):
        raise NotImplementedError("neighbourhood tile geometries are not monotone in the tile index")
    thresholds = [case_of_tile.index(ci) for ci in range(1, len(cases))]
    ridx = np.stack([c[1] for c in cases])
    valid = np.stack([c[2] for c in cases])
    return band_rows, ridx, valid, thresholds


def _na_bias(rpb, ridx, valid):
    col = jnp.arange(GRID_W)
    col_start = jnp.clip(col - WIN_C // 2, 0, GRID_W - WIN_C)
    in_win = (col[None, :] >= col_start[:, None]) & (col[None, :] < col_start[:, None] + WIN_C)
    col_off = jnp.clip(col[None, :] - col[:, None] + WIN_C - 1, 0, 2 * WIN_C - 2)
    col_bias = jnp.where(in_win[None, None], rpb[:, :, col_off].astype(F32) * LOG2_E, NEG_INF)
    masked = jnp.full_like(col_bias[:, :1], NEG_INF)
    col_bias_q = jnp.transpose(jnp.concatenate([col_bias, masked], axis=1), (0, 2, 1, 3))
    n_case, rt, band_rows = ridx.shape
    sel = np.where(valid, ridx, 2 * WIN_R - 1)
    heads = rpb.shape[0]
    cases = []
    for ci in range(n_case):
        rows = [jnp.take(col_bias_q, jnp.asarray(sel[ci, qr]), axis=2).reshape(heads, GRID_W, band_rows * GRID_W)
                for qr in range(rt)]
        cases.append(jnp.concatenate(rows, axis=1))
    return jnp.stack(cases, axis=1)


def kernel(x_prompt, x_sample, cache_gqa_k, cache_gqa_v, cache_mla_ckv, cache_mla_kpe, cache_diff_k, cache_diff_v, cache_na_k, cache_na_v, c, c_ctx, ada_w, ada_b, norm_mix_pre, norm_mix_post, norm_ffn_pre, norm_ffn_post, ffn_w_gu, ffn_w_down, gqa_w_qkv, gqa_q_norm, gqa_k_norm, gqa_w_o, mla_w_in, mla_q_norm, mla_kv_norm, mla_w_q_up, mla_w_kv_up, mla_w_o, diff_w_qkv, diff_lam_q1, diff_lam_k1, diff_lam_q2, diff_lam_k2, diff_subln, diff_w_o, na_w_qkv, na_rpb, na_w_o):
    nb, seq, d = x_prompt.shape
    db, dseq, _ = x_sample.shape
    past = cache_gqa_k.shape[2]
    depth = ada_w.shape[0]
    tp, ts = nb * seq, db * dseq
    t = tp + ts
    hd = HEAD_DIM
    assert db + 1 <= MOD_ROWS and dseq % GRID_W == 0
    assert tp % dseq == 0 and t % past == 0 and tp % past == 0

    tr = _pick(math.gcd(seq, dseq), 256, 8)
    n_ctx_tiles, tiles_per_dec = tp // tr, dseq // tr

    def seg_of_tile_at(layer):
        return lambda i: layer * MOD_ROWS + jnp.where(i < n_ctx_tiles, 0, 1 + (i - n_ctx_tiles) // tiles_per_dec)

    cond = jnp.concatenate([c_ctx[None, :], c, jnp.zeros((MOD_ROWS - 1 - db, d), F32)], axis=0)
    mods = _adaln(cond, ada_w, ada_b).reshape(depth * MOD_ROWS, 1, 6 * d)

    x = (x_prompt.reshape(tp, d), x_sample.reshape(ts, d))
    o_buf = [None]

    rope128 = _rope_tables(tp, db, dseq, hd)
    rope64 = _rope_tables(tp, db, dseq, B_ROPE_DIM)
    lat_blk0 = tp // dseq
    tq_dense = _pick(dseq, 512, GRID_W)

    def vec(v):
        return v.reshape(1, -1).astype(F32)

    def ctx_out(arr, shape):
        return arr[:tp].reshape((nb, seq) + shape)

    def attention(q, ctx_part, lat_parts, *, heads, kv_share, n_maps, dk, dv, hpb_ctx, hpb_lat, tq_lat, name,
                  fill=_fill_kv, head_operands=(), bias=None, band=None, lambda_init=0.0):
        def part_operands(part, hpb):
            slots = hpb // kv_share
            ops = []
            for arr, rows, blk_of, cols in part:
                if cols is None:
                    ops.append((arr, (rows, LANES), lambda b, g, i, blk_of=blk_of: (blk_of(b), 0)))
                else:
                    ops.append((arr, (rows, slots * cols), lambda b, g, i, blk_of=blk_of: (blk_of(b), g)))
            return ops

        def cfg(hpb, n_parts, band_cfg):
            return dict(hpb=hpb, kv_share=kv_share, n_maps=n_maps, dk=dk, dv=dv, n_parts=n_parts,
                        refs_per_part=len(ctx_part), fill=fill, band=band_cfg, lambda_init=lambda_init)

        qw = n_maps * dk
        hpb_c = min(hpb_ctx, heads)
        ops = [(a, blk, (lambda b, g, i: (0, 0))) for a, blk in head_operands]
        ops.append((q, (seq, hpb_c * qw), lambda b, g, i: (b, g)))
        ops += part_operands(ctx_part, hpb_c)
        buf = o_buf[0]
        if buf is None or buf.shape != (t, heads * dv):
            buf = jnp.zeros((t, heads * dv), BF16)
        o = _attention_call((nb, heads // hpb_c, 1), ops, t_rows=t, heads=heads, tq=seq,
                            q_tile_of=lambda b, i: b, key_rows=seq, cfg=cfg(hpb_c, 1, None), name=name + "_ctx",
                            into=buf)
        hpb_l = min(hpb_lat, heads)
        n_qt, ctx_tiles = dseq // tq_lat, tp // tq_lat

        def q_tile_of(b, i):
            return ctx_tiles + b * n_qt + i

        ops = [(a, blk, (lambda b, g, i: (0, 0))) for a, blk in head_operands]
        ops.append((q, (tq_lat, hpb_l * qw), lambda b, g, i: (q_tile_of(b, i), g)))
        for part in lat_parts:
            ops += part_operands(part, hpb_l)
        if bias is not None:
            bias_arr, case_of = bias
            ops.append((bias_arr, (hpb_l, None) + bias_arr.shape[2:], lambda b, g, i: (g, case_of(i), 0, 0)))
        return _attention_call((db, heads // hpb_l, n_qt), ops, t_rows=t, heads=heads, tq=tq_lat,
                               q_tile_of=q_tile_of, key_rows=past + dseq, cfg=cfg(hpb_l, len(lat_parts), band),
                               name=name + "_lat", into=o)

    def kv_parts(k, v, ck, cv, cols_k, cols_v):
        ctx = [(k, seq, lambda b: b, cols_k), (v, seq, lambda b: b, cols_v)]
        lat = [[(ck, past, lambda b: b, cols_k), (cv, past, lambda b: b, cols_v)],
               [(k, dseq, lambda b: lat_blk0 + b, cols_k), (v, dseq, lambda b: lat_blk0 + b, cols_v)]]
        return ctx, lat

    outs = {k: [] for k in ("gqa_k", "gqa_v", "mla_ckv", "mla_kpe", "diff_k", "diff_v", "na_k", "na_v")}

    ffn_down_bf16 = ffn_w_down.astype(BF16)

    (h,) = _rows(x, None, mods, tr, pre=(vec(norm_mix_pre[0]), seg_of_tile_at(0), 0, 1))

    for layer in range(depth):
        kind, j = layer % N_MIXERS, layer // N_MIXERS
        seg = seg_of_tile_at(layer)

        if kind == 0:
            w = gqa_w_qkv[j].astype(BF16)
            n_q = gqa_w_o.shape[1]
            n_kv = (w.shape[1] - n_q) // 2
            heads, kv_heads = n_q // hd, n_kv // hd
            group = heads // kv_heads
            q_scale = hd ** -0.5 * LOG2_E
            q = _matmul(h, w, BF16, w_cols=(0, n_q), gain=jnp.tile(vec(gqa_q_norm[j]), (1, heads)), norm_width=hd,
                        rope_tabs=rope128, rope_half=hd // 2, out_scale=q_scale)
            k = _matmul(h, w, F32, w_cols=(n_q, n_kv), gain=jnp.tile(vec(gqa_k_norm[j]), (1, kv_heads)),
                        norm_width=hd, rope_tabs=rope128, rope_half=hd // 2)
            v = _matmul(h, w, F32, w_cols=(n_q + n_kv, n_kv))
            outs["gqa_k"].append(ctx_out(k, (kv_heads, hd)))
            outs["gqa_v"].append(ctx_out(v, (kv_heads, hd)))
            ck = cache_gqa_k[:, j].reshape(db * past, n_kv)
            cv = cache_gqa_v[:, j].reshape(db * past, n_kv)
            ctx_part, lat_parts = kv_parts(k, v, ck, cv, hd, hd)
            o = attention(q, ctx_part, lat_parts, heads=heads, kv_share=group, n_maps=1, dk=hd, dv=hd,
                          hpb_ctx=2 * group, hpb_lat=group, tq_lat=tq_dense, name="attn_gqa")
            w_o = gqa_w_o[j]

        elif kind == 1:
            q_rank, kv_rank = mla_q_norm.shape[1], mla_kv_norm.shape[1]
            w_in = mla_w_in[j].astype(BF16)
            rope_dim = w_in.shape[1] - q_rank - kv_rank
            w_kpe = jnp.pad(w_in[:, q_rank + kv_rank:], ((0, 0), (0, LANES - rope_dim)))
            heads = mla_w_o.shape[1] // hd
            q_scale = (hd + rope_dim) ** -0.5 * LOG2_E
            cq = _matmul(h, w_in, BF16, w_cols=(0, q_rank), gain=vec(mla_q_norm[j]), norm_width=q_rank)
            ckv = _matmul(h, w_in, F32, w_cols=(q_rank, kv_rank), gain=vec(mla_kv_norm[j]), norm_width=kv_rank)
            kpe = _matmul(h, w_kpe, F32, rope_tabs=rope64, rope_half=rope_dim // 2)
            outs["mla_ckv"].append(ctx_out(ckv, (kv_rank,)))
            outs["mla_kpe"].append(ctx_out(kpe[:, :rope_dim], (rope_dim,)))
            w_q = jnp.pad(mla_w_q_up[j].astype(BF16).reshape(q_rank, heads, hd + rope_dim),
                          ((0, 0), (0, 0), (0, LANES - rope_dim))).reshape(q_rank, heads * 2 * hd)
            q = _matmul(cq, w_q, BF16, rope_tabs=rope64, rope_half=rope_dim // 2, rope_every=2, out_scale=q_scale)
            ckv_all = jnp.concatenate([ckv, cache_mla_ckv[:, j].reshape(db * past, kv_rank)], axis=0)
            kv = _matmul(ckv_all, mla_w_kv_up[j].astype(BF16), BF16)
            ckpe = jnp.pad(cache_mla_kpe[:, j].reshape(db * past, rope_dim), ((0, 0), (0, LANES - rope_dim)))
            cache_blk0 = t // past
            ctx_part = [(kv, seq, lambda b: b, 2 * hd), (kpe, seq, lambda b: b, None)]
            lat_parts = [[(kv, past, lambda b: cache_blk0 + b, 2 * hd), (ckpe, past, lambda b: b, None)],
                         [(kv, dseq, lambda b: lat_blk0 + b, 2 * hd), (kpe, dseq, lambda b: lat_blk0 + b, None)]]
            o = attention(q, ctx_part, lat_parts, heads=heads, kv_share=1, n_maps=1, dk=2 * hd, dv=hd,
                          hpb_ctx=8, hpb_lat=4, tq_lat=tq_dense, name="attn_mla", fill=_fill_mla)
            w_o = mla_w_o[j]

        elif kind == 2:
            lambda_init = 0.8 - 0.6 * math.exp(-0.3 * layer)
            w = diff_w_qkv[j].astype(BF16)
            n_w = w.shape[1] // 3
            heads = n_w // (2 * hd)
            q_scale = hd ** -0.5 * LOG2_E
            q = _matmul(h, w, BF16, w_cols=(0, n_w), rope_tabs=rope128, rope_half=hd // 2, out_scale=q_scale)
            k = _matmul(h, w, F32, w_cols=(n_w, n_w), rope_tabs=rope128, rope_half=hd // 2)
            v = _matmul(h, w, F32, w_cols=(2 * n_w, n_w))
            outs["diff_k"].append(ctx_out(k, (heads, 2, hd)))
            outs["diff_v"].append(ctx_out(v, (heads, 2 * hd)))
            ck = cache_diff_k[:, j].reshape(db * past, n_w)
            cv = cache_diff_v[:, j].reshape(db * past, n_w)
            lam = jnp.stack([diff_lam_q1[j], diff_lam_k1[j], diff_lam_q2[j], diff_lam_k2[j]]).astype(F32)
            ctx_part, lat_parts = kv_parts(k, v, ck, cv, 2 * hd, 2 * hd)
            o = attention(q, ctx_part, lat_parts, heads=heads, kv_share=1, n_maps=2, dk=hd, dv=2 * hd,
                          hpb_ctx=4, hpb_lat=2, tq_lat=tq_dense, name="attn_diff",
                          head_operands=[(lam, (4, hd)), (vec(diff_subln[j]), (1, 2 * hd))], lambda_init=lambda_init)
            w_o = diff_w_o[j]

        else:
            w = na_w_qkv[j].astype(BF16)
            n_w = w.shape[1] // 3
            heads = n_w // hd
            q_scale = hd ** -0.5 * LOG2_E
            q = _matmul(h, w, BF16, w_cols=(0, n_w), out_scale=q_scale)
            k = _matmul(h, w, F32, w_cols=(n_w, n_w))
            v = _matmul(h, w, F32, w_cols=(2 * n_w, n_w))
            outs["na_k"].append(ctx_out(k, (heads, hd)))
            outs["na_v"].append(ctx_out(v, (heads, hd)))
            ck = cache_na_k[:, j].reshape(db * past, n_w)
            cv = cache_na_v[:, j].reshape(db * past, n_w)
            n_rows = dseq // GRID_W
            tq_na = _pick(dseq, 256, GRID_W)
            band_rows, ridx, valid, thresholds = _na_band_plan(n_rows, tq_na)
            bias = _na_bias(na_rpb[j], ridx, valid)

            def case_of(i):
                return sum((i >= th).astype(jnp.int32) for th in thresholds) if thresholds else 0

            ctx_part, lat_parts = kv_parts(k, v, ck, cv, hd, hd)
            o = attention(q, ctx_part, lat_parts, heads=heads, kv_share=1, n_maps=1, dk=hd, dv=hd,
                          hpb_ctx=8, hpb_lat=4, tq_lat=tq_na, name="attn_na", bias=(bias, case_of),
                          band=(n_rows, band_rows))
            w_o = na_w_o[j]

        y = _matmul(o, w_o.astype(BF16), F32)
        o_buf[0] = o
        x, h = _rows(x, y, mods, tr, post=(vec(norm_mix_post[layer]), seg, 2),
                     pre=(vec(norm_ffn_pre[layer]), seg, 3, 4))

        act = _matmul(h, ffn_w_gu, BF16, layer=layer, swiglu=True, bm=2048, bn=256, single_buffer_a=True)
        y = _matmul(act, ffn_down_bf16, F32, layer=layer, bm=512, bn=512)
        post = (vec(norm_ffn_post[layer]), seg, 5)
        if layer + 1 < depth:
            x, h = _rows(x, y, mods, tr, post=post,
                         pre=(vec(norm_mix_pre[layer + 1]), seg_of_tile_at(layer + 1), 0, 1))
        else:
            (y_prompt,) = _rows(x, y, mods, tr, post=post, tile0=0, n_tiles=n_ctx_tiles)
            (y_sample,) = _rows(x, y, mods, tr, post=post, tile0=n_ctx_tiles, n_tiles=ts // tr)

    return (y_prompt.reshape(nb, seq, d), y_sample.reshape(db, dseq, d),
            jnp.stack(outs["gqa_k"], axis=1), jnp.stack(outs["gqa_v"], axis=1),
            jnp.stack(outs["mla_ckv"], axis=1), jnp.stack(outs["mla_kpe"], axis=1),
            jnp.stack(outs["diff_k"], axis=1), jnp.stack(outs["diff_v"], axis=1),
            jnp.stack(outs["na_k"], axis=1), jnp.stack(outs["na_v"], axis=1))
```

```python
import functools
import math

import numpy as np
import jax
import jax.numpy as jnp
from jax import lax
from jax.experimental import pallas as pl
from jax.experimental.pallas import tpu as pltpu

GRID_W = 64
N_MIXERS = 4
ROPE_BASE = 10000.0
NORM_EPS = 1e-6
NEG_INF = -1e30
HEAD_DIM = 128
B_ROPE_DIM = 64
WIN_R = 8
WIN_C = 16
LOG2_E = math.log2(math.e)

LANES = 128
MXU_COLS = 256
MOD_ROWS = 16
V7X_VMEM_BYTES = 64 * 1024 * 1024
VMEM_CAP_BYTES = V7X_VMEM_BYTES - 8 * 1024 * 1024

BF16 = jnp.bfloat16
F32 = jnp.float32


def _pick(n, target, mult):
    best = None
    for t in range(mult, min(n, target) + 1, mult):
        if n % t == 0:
            best = t
    return n if best is None else best


def _params(vmem_estimate_bytes, n_grid):
    limit = int(min(VMEM_CAP_BYTES, max(32 * 1024 * 1024, vmem_estimate_bytes)))
    return pltpu.CompilerParams(dimension_semantics=("arbitrary",) * n_grid, vmem_limit_bytes=limit)


def _adaln_kernel(c_ref, w_ref, b_ref, o_ref):
    c = c_ref[...]
    a = (c * jax.nn.sigmoid(c)).astype(BF16)
    o_ref[...] = jnp.dot(a, w_ref[...].astype(BF16), preferred_element_type=F32) + b_ref[...]


def _adaln(cond, ada_w, ada_b):
    depth, d, n = ada_w.shape
    tn = _pick(n, 512, LANES)
    est = 2 * d * tn * 4 + d * tn * 2 + 4 * MOD_ROWS * (d + tn) * 4
    return pl.pallas_call(
        _adaln_kernel,
        grid=(depth, n // tn),
        in_specs=[
            pl.BlockSpec((MOD_ROWS, d), lambda l, j: (0, 0)),
            pl.BlockSpec((None, d, tn), lambda l, j: (l, 0, j)),
            pl.BlockSpec((None, 1, tn), lambda l, j: (l, 0, j)),
        ],
        out_specs=pl.BlockSpec((None, MOD_ROWS, tn), lambda l, j: (l, 0, j)),
        out_shape=jax.ShapeDtypeStruct((depth, MOD_ROWS, n), F32),
        compiler_params=_params(est + (8 << 20), 2),
        name="adaln",
    )(cond, ada_w, ada_b.reshape(depth, 1, n))


def _rms(x, g):
    return x * lax.rsqrt(jnp.mean(x * x, axis=-1, keepdims=True) + NORM_EPS) * g


def _rows_kernel(*refs, has_post, has_pre, split_tiles):
    refs = list(refs)
    x = refs.pop(0)[...]
    if split_tiles is not None:
        x = jnp.where(pl.program_id(0) < split_tiles, x, refs.pop(0)[...])
    if has_post:
        y_ref, gpost_ref, gate_ref = refs[:3]
        refs = refs[3:]
        x = x + gate_ref[0] * _rms(y_ref[...], gpost_ref[...])
    if has_pre:
        gpre_ref, shift_ref, scale_ref = refs[:3]
        refs = refs[3:]
    if has_post:
        refs.pop(0)[...] = x
    if has_pre:
        refs.pop(0)[...] = (_rms(x, gpre_ref[...]) * (1.0 + scale_ref[0]) + shift_ref[0]).astype(BF16)


def _rows(x, y, mods, tr, post=None, pre=None, tile0=0, n_tiles=None):
    xs = x if isinstance(x, tuple) else (x,)
    t, d = sum(a.shape[0] for a in xs), xs[0].shape[1]
    row_spec = pl.BlockSpec((tr, d), lambda i: (i + tile0, 0))
    split_tiles = None
    args, in_specs = [x], [row_spec]
    if isinstance(x, tuple):
        assert tile0 == 0
        split_tiles = xs[0].shape[0] // tr
        args = list(xs)
        in_specs = [pl.BlockSpec((tr, d), lambda i: (jnp.minimum(i, split_tiles - 1), 0)),
                    pl.BlockSpec((tr, d), lambda i: (jnp.maximum(i - split_tiles, 0), 0))]
    n_tiles = t // tr if n_tiles is None else n_tiles
    out_spec = pl.BlockSpec((tr, d), lambda i: (i, 0))
    vec_spec = pl.BlockSpec((1, d), lambda i: (0, 0))

    def mod_spec(seg_of_tile, slot):
        return pl.BlockSpec((1, 1, d), lambda i: (seg_of_tile(i + tile0), 0, slot))

    out_shape, out_specs = [], []
    if post is not None:
        args += [y, post[0], mods]
        in_specs += [row_spec, vec_spec, mod_spec(post[1], post[2])]
        out_shape.append(jax.ShapeDtypeStruct((n_tiles * tr, d), F32))
        out_specs.append(out_spec)
    if pre is not None:
        args += [pre[0], mods, mods]
        in_specs += [vec_spec, mod_spec(pre[1], pre[2]), mod_spec(pre[1], pre[3])]
        out_shape.append(jax.ShapeDtypeStruct((n_tiles * tr, d), BF16))
        out_specs.append(out_spec)
    est = 2 * tr * d * 4 * (len(args) + 2) + (8 << 20)
    out = pl.pallas_call(
        functools.partial(_rows_kernel, has_post=post is not None, has_pre=pre is not None, split_tiles=split_tiles),
        grid=(n_tiles,),
        in_specs=in_specs,
        out_specs=out_specs,
        out_shape=out_shape,
        compiler_params=_params(est, 1),
        name="rows",
    )(*args)
    return out


def _rope_lanes(y, tabs, half):
    if half == LANES // 2:
        return y * tabs[0] + pltpu.roll(y, LANES // 2, 1) * tabs[1]
    return (y * tabs[0] + pltpu.roll(y, LANES - half, 1) * tabs[1] + pltpu.roll(y, half, 1) * tabs[2])


def _mm_kernel(*refs, nk, swiglu, norm_width, rope_half, rope_every, out_scale, sub_n, n_out):
    refs = list(refs)
    a_ref = refs.pop(0)
    w_ref = refs.pop(0)
    w2_ref = refs.pop(0) if swiglu else None
    gain_ref = refs.pop(0) if norm_width else None
    n_tab = 0 if not rope_half else (2 if rope_half == LANES // 2 else 3)
    tab_refs = [refs.pop(0) for _ in range(n_tab)]
    out_refs = [refs.pop(0) for _ in range(n_out)]
    o_ref = out_refs[0]
    acc_refs = refs

    a = a_ref[...].astype(BF16)
    bn = o_ref.shape[-1]

    def epilogue(vals, col0):
        y = vals[0]
        if swiglu:
            y = (y * jax.nn.sigmoid(y)) * vals[1]
        if norm_width or rope_half:
            tabs = [r[...] for r in tab_refs]
            width = norm_width if norm_width else LANES
            for c in range(y.shape[-1] // width):
                sl = slice(col0 + c * width, col0 + (c + 1) * width)
                yc = y[:, c * width:(c + 1) * width]
                if norm_width:
                    yc = _rms(yc, gain_ref[:, sl])
                if rope_half and c % rope_every == rope_every - 1:
                    yc = _rope_lanes(yc, tabs, rope_half)
                if out_scale != 1.0:
                    yc = yc * out_scale
                for r in out_refs:
                    r[:, sl] = yc.astype(r.dtype)
        else:
            if out_scale != 1.0:
                y = y * out_scale
            for r in out_refs:
                r[:, col0:col0 + y.shape[-1]] = y.astype(r.dtype)

    if nk == 1 and sub_n < bn:
        for c0 in range(0, bn, sub_n):
            epilogue([jnp.dot(a, w_ref[:, c0:c0 + sub_n].astype(BF16), preferred_element_type=F32)], c0)
        return

    parts = [jnp.dot(a, w_ref[...].astype(BF16), preferred_element_type=F32)]
    if swiglu:
        parts.append(jnp.dot(a, w2_ref[...].astype(BF16), preferred_element_type=F32))

    if nk == 1:
        epilogue(parts, 0)
    else:
        k = pl.program_id(2)

        @pl.when(k == 0)
        def _():
            for r, p in zip(acc_refs, parts):
                r[...] = p

        @pl.when(k > 0)
        def _():
            for r, p in zip(acc_refs, parts):
                r[...] += p

        @pl.when(k == nk - 1)
        def _():
            epilogue([r[...] for r in acc_refs], 0)


def _matmul(a, w, out_dtype, *, layer=0, w_cols=None, swiglu=False, gain=None, norm_width=0, rope_tabs=None,
            rope_half=0, rope_every=1, out_scale=1.0, bm=1024, bn=None, bk=None, single_buffer_a=False,
            bf16_copy=False):
    m, kdim = a.shape
    col0, n = (0, w.shape[-1]) if w_cols is None else w_cols
    if swiglu:
        col0, n = 0, w.shape[-1] // 2
    bm = _pick(m, bm, 8)
    if bn is None:
        bn = 512 if (swiglu or bk) else 1024
    bn = _pick(math.gcd(n, col0) if col0 else n, bn, LANES * rope_every)
    if norm_width > LANES:
        bn = norm_width
    bk = kdim if bk is None else _pick(kdim, bk, LANES)
    nk = kdim // bk
    jb = col0 // bn
    sub_n = bn
    if (norm_width or rope_half) and norm_width <= MXU_COLS and nk == 1:
        sub_n = _pick(bn, MXU_COLS, LANES * rope_every)

    def w_spec(col_blk0):
        if w.ndim == 3:
            return pl.BlockSpec((None, bk, bn), lambda i, j, k: (layer, k, j + col_blk0))
        return pl.BlockSpec((bk, bn), lambda i, j, k: (k, j + col_blk0))

    a_buffers = 1 if (single_buffer_a and nk == 1) else 2
    a_mode = dict(pipeline_mode=pl.Buffered(1)) if a_buffers == 1 else {}
    args = [a, w]
    in_specs = [pl.BlockSpec((bm, bk), lambda i, j, k: (i, k), **a_mode), w_spec(jb)]
    if swiglu:
        args.append(w)
        in_specs.append(w_spec(n // bn))
    if norm_width:
        args.append(gain)
        in_specs.append(pl.BlockSpec((1, bn), lambda i, j, k: (0, j)))
    if rope_half:
        for tab in rope_tabs:
            args.append(tab)
            in_specs.append(pl.BlockSpec((bm, LANES), lambda i, j, k: (i, 0)))
    n_w = 2 if swiglu else 1
    scratch = [pltpu.VMEM((bm, bn), F32) for _ in range(n_w)] if nk > 1 else []
    out_dtypes = [out_dtype] + ([BF16] if bf16_copy else [])
    est = (a_buffers * bm * bk * a.dtype.itemsize + (bm * bk * 2 if a.dtype != BF16 else 0)
           + 2 * n_w * bk * bn * w.dtype.itemsize + (n_w * bk * bn * 2 if w.dtype != BF16 else 0)
           + sum(2 * bm * bn * jnp.dtype(dt).itemsize for dt in out_dtypes) + (n_w + 2) * bm * bn * 4 + (4 << 20))
    out = pl.pallas_call(
        functools.partial(_mm_kernel, nk=nk, swiglu=swiglu, norm_width=norm_width, rope_half=rope_half,
                          rope_every=rope_every, out_scale=out_scale, sub_n=sub_n, n_out=len(out_dtypes)),
        grid=(m // bm, n // bn, nk),
        in_specs=in_specs,
        out_specs=[pl.BlockSpec((bm, bn), lambda i, j, k: (i, j)) for _ in out_dtypes],
        out_shape=[jax.ShapeDtypeStruct((m, n), dt) for dt in out_dtypes],
        scratch_shapes=scratch,
        compiler_params=_params(est, 3),
        name="matmul",
    )(*args)
    return tuple(out) if bf16_copy else out[0]


def _dot_nt(q, k):
    return lax.dot_general(q, k, (((1,), (1,)), ((), ())), preferred_element_type=F32)


def _softmax_parts(s_parts):
    m = s_parts[0].max(axis=-1, keepdims=True)
    for s in s_parts[1:]:
        m = jnp.maximum(m, s.max(axis=-1, keepdims=True))
    e_parts = [jnp.exp2(s - m) for s in s_parts]
    denom = e_parts[0].sum(axis=-1, keepdims=True)
    for e in e_parts[1:]:
        denom = denom + e.sum(axis=-1, keepdims=True)
    return e_parts, denom


def _fill_kv(part_refs, kbf, vbf, row0, n_rows):
    k_ref, v_ref = part_refs
    kbf[row0:row0 + n_rows, :] = k_ref[...].astype(BF16)
    vbf[row0:row0 + n_rows, :] = v_ref[...].astype(BF16)


def _fill_mla(part_refs, kbf, vbf, row0, n_rows):
    kv_ref, kpe_ref = part_refs
    d = HEAD_DIM
    kpe = kpe_ref[...].astype(BF16)
    for slot in range(kv_ref.shape[1] // (2 * d)):
        kbf[row0:row0 + n_rows, 2 * slot * d:(2 * slot + 1) * d] = kv_ref[:, 2 * slot * d:(2 * slot + 1) * d]
        kbf[row0:row0 + n_rows, (2 * slot + 1) * d:(2 * slot + 2) * d] = kpe
        vbf[row0:row0 + n_rows, slot * d:(slot + 1) * d] = kv_ref[:, (2 * slot + 1) * d:(2 * slot + 2) * d]


def _attn_kernel(*refs, hpb, kv_share, n_maps, dk, dv, n_parts, refs_per_part, fill, band, lambda_init):
    refs = list(refs)
    if n_maps == 2:
        lam_ref, gain_ref = refs.pop(0), refs.pop(0)
    q_ref = refs.pop(0)
    part_refs = [tuple(refs.pop(0) for _ in range(refs_per_part)) for _ in range(n_parts)]
    bias_ref = refs.pop(0) if band else None
    o_ref = refs.pop(0)
    kbf, vbf = refs
    tq = q_ref.shape[0]
    part_rows = [p[0].shape[0] for p in part_refs]

    @pl.when(pl.program_id(2) == 0)
    def _():
        row0 = 0
        for p, n in zip(part_refs, part_rows):
            fill(p, kbf, vbf, row0, n)
            row0 += n

    if band:
        n_rows, band_rows = band
        wr = min(WIN_R, n_rows)
        r0 = pl.program_id(2) * (tq // GRID_W)
        assert part_rows[0] % GRID_W == 0
        start = part_rows[0] + jnp.clip(r0 - wr // 2, 0, n_rows - band_rows) * GRID_W
        key_rows = [pl.ds(0, part_rows[0]), pl.ds(pl.multiple_of(start, GRID_W), band_rows * GRID_W)]
    else:
        key_rows = [pl.ds(0, sum(part_rows))]

    if n_maps == 2:
        lam = (jnp.exp(jnp.sum(lam_ref[0:1, :] * lam_ref[1:2, :], axis=-1, keepdims=True))
               - jnp.exp(jnp.sum(lam_ref[2:3, :] * lam_ref[3:4, :], axis=-1, keepdims=True)) + lambda_init)

    for hs in range(hpb):
        slot = hs // kv_share
        maps = []
        for mp in range(n_maps):
            q = q_ref[:, (hs * n_maps + mp) * dk:(hs * n_maps + mp + 1) * dk]
            kcols = slice((slot * n_maps + mp) * dk, (slot * n_maps + mp + 1) * dk)
            s_parts = [_dot_nt(q, kbf[rows, kcols]) for rows in key_rows]
            if band:
                s_parts[-1] = s_parts[-1] + bias_ref[hs]
            maps.append(_softmax_parts(s_parts))
        vcols = slice(slot * dv, (slot + 1) * dv)
        if n_maps == 1:
            (e_parts, denom), = maps
            wgts = [e.astype(BF16) for e in e_parts]
        else:
            (e0, l0), (e1, l1) = maps
            ratio = lam * l0 / l1
            wgts = [(a - b * ratio).astype(BF16) for a, b in zip(e0, e1)]
            denom = l0
        o = jnp.dot(wgts[0], vbf[key_rows[0], vcols], preferred_element_type=F32)
        for wgt, rows in zip(wgts[1:], key_rows[1:]):
            o = o + jnp.dot(wgt, vbf[rows, vcols], preferred_element_type=F32)
        o = o / denom
        if n_maps == 2:
            o = _rms(o, gain_ref[...]) * (1.0 - lambda_init)
        o_ref[:, hs * dv:(hs + 1) * dv] = o.astype(o_ref.dtype)


def _drop_ref(kernel_fn, index):
    def wrapped(*refs):
        return kernel_fn(*refs[:index], *refs[index + 1:])
    return wrapped


def _attention_call(grid, operands, *, t_rows, heads, tq, q_tile_of, key_rows, cfg, name, into=None):
    hpb, dv, n_maps, dk = cfg["hpb"], cfg["dv"], cfg["n_maps"], cfg["dk"]
    slots = hpb // cfg["kv_share"]
    in_specs = [pl.BlockSpec(blk, imap) for _, blk, imap in operands]
    args = [arr for arr, _, _ in operands]
    kernel_fn = functools.partial(_attn_kernel, **cfg)
    aliases = {}
    if into is not None:
        kernel_fn = _drop_ref(kernel_fn, len(args))
        aliases = {len(args): 0}
        in_specs.append(pl.BlockSpec(memory_space=pl.ANY))
        args.append(into)
    est = (hpb * n_maps * tq * key_rows * 10 + 3 * key_rows * slots * (n_maps * dk + dv) * 4
           + 4 * tq * hpb * (n_maps * dk + dv) * 2 + (8 << 20))
    return pl.pallas_call(
        kernel_fn,
        grid=grid,
        in_specs=in_specs,
        out_specs=pl.BlockSpec((tq, hpb * dv), lambda b, g, i: (q_tile_of(b, i), g)),
        out_shape=jax.ShapeDtypeStruct((t_rows, heads * dv), BF16),
        scratch_shapes=[pltpu.VMEM((key_rows, slots * n_maps * dk), BF16),
                        pltpu.VMEM((key_rows, slots * dv), BF16)],
        input_output_aliases=aliases,
        compiler_params=_params(est, 3),
        name=name,
    )(*args)


def _rope_tables(n_ctx_rows, dec_batch, dec_seq, rot_dim):
    t = jnp.arange(dec_seq)
    per_axis = rot_dim // 2
    inv_freq = ROPE_BASE ** (-jnp.arange(0, per_axis, 2, dtype=F32) / per_axis)
    row = (t // GRID_W).astype(F32)[:, None] * inv_freq
    col = (t % GRID_W).astype(F32)[:, None] * inv_freq
    ang = jnp.concatenate([row, col], axis=-1)
    cos, sin = jnp.cos(ang), jnp.sin(ang)
    pad = LANES - rot_dim
    ones = jnp.ones((dec_seq, pad), F32)
    zeros_half = jnp.zeros_like(sin)
    zeros_pad = jnp.zeros((dec_seq, pad), F32)
    if pad == 0:
        tabs = [jnp.concatenate([cos, cos], -1), jnp.concatenate([-sin, sin], -1)]
    else:
        tabs = [jnp.concatenate([cos, cos, ones], -1),
                jnp.concatenate([-sin, zeros_half, zeros_pad], -1),
                jnp.concatenate([zeros_half, sin, zeros_pad], -1)]
    out = []
    for i, tab in enumerate(tabs):
        ctx = jnp.ones((n_ctx_rows, LANES), F32) if i == 0 else jnp.zeros((n_ctx_rows, LANES), F32)
        out.append(jnp.concatenate([ctx, jnp.tile(tab, (dec_batch, 1))], axis=0))
    return out


def _na_band_plan(n_rows, tq):
    wr = min(WIN_R, n_rows)
    rt = tq // GRID_W
    rows_per_lane_tile = LANES // GRID_W
    band_rows = min(n_rows, -(-(wr + rt - 1) // rows_per_lane_tile) * rows_per_lane_tile)
    cases, case_of_tile = [], []
    for qt in range(n_rows // rt):
        r0 = qt * rt
        bs = int(np.clip(r0 - wr // 2, 0, n_rows - band_rows))
        r = r0 + np.arange(rt)[:, None]
        kr = bs + np.arange(band_rows)[None, :]
        rs = np.clip(r - wr // 2, 0, n_rows - wr)
        valid = (kr >= rs) & (kr < rs + wr)
        ridx = np.clip(kr - r + WIN_R - 1, 0, 2 * WIN_R - 2)
        key = (valid.tobytes(), ridx.tobytes())
        for ci, (k0, _, _) in enumerate(cases):
            if k0 == key:
                break
        else:
            ci = len(cases)
            cases.append((key, ridx, valid))
        case_of_tile.append(ci)
    if any(b < a for a, b in zip(case_of_tile, case_of_tile[1:])) or \
            sorted(set(case_of_tile)) != list(range(len(cases))):
        raise NotImplementedError("neighbourhood tile geometries are not monotone in the tile index")
    thresholds = [case_of_tile.index(ci) for ci in range(1, len(cases))]
    ridx = np.stack([c[1] for c in cases])
    valid = np.stack([c[2] for c in cases])
    return band_rows, ridx, valid, thresholds


def _na_bias(rpb, ridx, valid):
    col = jnp.arange(GRID_W)
    col_start = jnp.clip(col - WIN_C // 2, 0, GRID_W - WIN_C)
    in_win = (col[None, :] >= col_start[:, None]) & (col[None, :] < col_start[:, None] + WIN_C)
    col_off = jnp.clip(col[None, :] - col[:, None] + WIN_C - 1, 0, 2 * WIN_C - 2)
    col_bias = jnp.where(in_win[None, None], rpb[:, :, col_off].astype(F32) * LOG2_E, NEG_INF)
    masked = jnp.full_like(col_bias[:, :1], NEG_INF)
    col_bias_q = jnp.transpose(jnp.concatenate([col_bias, masked], axis=1), (0, 2, 1, 3))
    n_case, rt, band_rows = ridx.shape
    sel = np.where(valid, ridx, 2 * WIN_R - 1)
    heads = rpb.shape[0]
    cases = []
    for ci in range(n_case):
        rows = [jnp.take(col_bias_q, jnp.asarray(sel[ci, qr]), axis=2).reshape(heads, GRID_W, band_rows * GRID_W)
                for qr in range(rt)]
        cases.append(jnp.concatenate(rows, axis=1))
    return jnp.stack(cases, axis=1)


def kernel(x_prompt, x_sample, cache_gqa_k, cache_gqa_v, cache_mla_ckv, cache_mla_kpe, cache_diff_k, cache_diff_v, cache_na_k, cache_na_v, c, c_ctx, ada_w, ada_b, norm_mix_pre, norm_mix_post, norm_ffn_pre, norm_ffn_post, ffn_w_gu, ffn_w_down, gqa_w_qkv, gqa_q_norm, gqa_k_norm, gqa_w_o, mla_w_in, mla_q_norm, mla_kv_norm, mla_w_q_up, mla_w_kv_up, mla_w_o, diff_w_qkv, diff_lam_q1, diff_lam_k1, diff_lam_q2, diff_lam_k2, diff_subln, diff_w_o, na_w_qkv, na_rpb, na_w_o):
    nb, seq, d = x_prompt.shape
    db, dseq, _ = x_sample.shape
    past = cache_gqa_k.shape[2]
    depth = ada_w.shape[0]
    tp, ts = nb * seq, db * dseq
    t = tp + ts
    hd = HEAD_DIM
    assert db + 1 <= MOD_ROWS and dseq % GRID_W == 0
    assert tp % dseq == 0 and t % past == 0 and tp % past == 0

    tr = _pick(math.gcd(seq, dseq), 256, 8)
    n_ctx_tiles, tiles_per_dec = tp // tr, dseq // tr

    def seg_of_tile_at(layer):
        return lambda i: layer * MOD_ROWS + jnp.where(i < n_ctx_tiles, 0, 1 + (i - n_ctx_tiles) // tiles_per_dec)

    cond = jnp.concatenate([c_ctx[None, :], c, jnp.zeros((MOD_ROWS - 1 - db, d), F32)], axis=0)
    mods = _adaln(cond, ada_w, ada_b).reshape(depth * MOD_ROWS, 1, 6 * d)

    x = (x_prompt.reshape(tp, d), x_sample.reshape(ts, d))
    o_buf = [None]

    rope128 = _rope_tables(tp, db, dseq, hd)
    rope64 = _rope_tables(tp, db, dseq, B_ROPE_DIM)
    lat_blk0 = tp // dseq
    tq_dense = _pick(dseq, 512, GRID_W)

    def vec(v):
        return v.reshape(1, -1).astype(F32)

    def ctx_out(arr, shape):
        return arr[:tp].reshape((nb, seq) + shape)

    def attention(q, ctx_part, lat_parts, *, heads, kv_share, n_maps, dk, dv, hpb_ctx, hpb_lat, tq_lat, name,
                  fill=_fill_kv, head_operands=(), bias=None, band=None, lambda_init=0.0):
        def part_operands(part, hpb):
            slots = hpb // kv_share
            ops = []
            for arr, rows, blk_of, cols in part:
                if cols is None:
                    ops.append((arr, (rows, LANES), lambda b, g, i, blk_of=blk_of: (blk_of(b), 0)))
                else:
                    ops.append((arr, (rows, slots * cols), lambda b, g, i, blk_of=blk_of: (blk_of(b), g)))
            return ops

        def cfg(hpb, n_parts, band_cfg):
            return dict(hpb=hpb, kv_share=kv_share, n_maps=n_maps, dk=dk, dv=dv, n_parts=n_parts,
                        refs_per_part=len(ctx_part), fill=fill, band=band_cfg, lambda_init=lambda_init)

        qw = n_maps * dk
        hpb_c = min(hpb_ctx, heads)
        ops = [(a, blk, (lambda b, g, i: (0, 0))) for a, blk in head_operands]
        ops.append((q, (seq, hpb_c * qw), lambda b, g, i: (b, g)))
        ops += part_operands(ctx_part, hpb_c)
        buf = o_buf[0]
        if buf is None or buf.shape != (t, heads * dv):
            buf = jnp.zeros((t, heads * dv), BF16)
        o = _attention_call((nb, heads // hpb_c, 1), ops, t_rows=t, heads=heads, tq=seq,
                            q_tile_of=lambda b, i: b, key_rows=seq, cfg=cfg(hpb_c, 1, None), name=name + "_ctx",
                            into=buf)
        hpb_l = min(hpb_lat, heads)
        n_qt, ctx_tiles = dseq // tq_lat, tp // tq_lat

        def q_tile_of(b, i):
            return ctx_tiles + b * n_qt + i

        ops = [(a, blk, (lambda b, g, i: (0, 0))) for a, blk in head_operands]
        ops.append((q, (tq_lat, hpb_l * qw), lambda b, g, i: (q_tile_of(b, i), g)))
        for part in lat_parts:
            ops += part_operands(part, hpb_l)
        if bias is not None:
            bias_arr, case_of = bias
            ops.append((bias_arr, (hpb_l, None) + bias_arr.shape[2:], lambda b, g, i: (g, case_of(i), 0, 0)))
        return _attention_call((db, heads // hpb_l, n_qt), ops, t_rows=t, heads=heads, tq=tq_lat,
                               q_tile_of=q_tile_of, key_rows=past + dseq, cfg=cfg(hpb_l, len(lat_parts), band),
                               name=name + "_lat", into=o)

    def kv_parts(k, v, ck, cv, cols_k, cols_v):
        ctx = [(k, seq, lambda b: b, cols_k), (v, seq, lambda b: b, cols_v)]
        lat = [[(ck, past, lambda b: b, cols_k), (cv, past, lambda b: b, cols_v)],
               [(k, dseq, lambda b: lat_blk0 + b, cols_k), (v, dseq, lambda b: lat_blk0 + b, cols_v)]]
        return ctx, lat

    outs = {k: [] for k in ("gqa_k", "gqa_v", "mla_ckv", "mla_kpe", "diff_k", "diff_v", "na_k", "na_v")}

    ffn_down_bf16 = ffn_w_down.astype(BF16)

    (h,) = _rows(x, None, mods, tr, pre=(vec(norm_mix_pre[0]), seg_of_tile_at(0), 0, 1))

    for layer in range(depth):
        kind, j = layer % N_MIXERS, layer // N_MIXERS
        seg = seg_of_tile_at(layer)

        if kind == 0:
            w = gqa_w_qkv[j].astype(BF16)
            n_q = gqa_w_o.shape[1]
            n_kv = (w.shape[1] - n_q) // 2
            heads, kv_heads = n_q // hd, n_kv // hd
            group = heads // kv_heads
            q_scale = hd ** -0.5 * LOG2_E
            q = _matmul(h, w, BF16, w_cols=(0, n_q), gain=jnp.tile(vec(gqa_q_norm[j]), (1, heads)), norm_width=hd,
                        rope_tabs=rope128, rope_half=hd // 2, out_scale=q_scale)
            k, k16 = _matmul(h, w, F32, w_cols=(n_q, n_kv), gain=jnp.tile(vec(gqa_k_norm[j]), (1, kv_heads)),
                             norm_width=hd, rope_tabs=rope128, rope_half=hd // 2, bf16_copy=True)
            v, v16 = _matmul(h, w, F32, w_cols=(n_q + n_kv, n_kv), bf16_copy=True)
            outs["gqa_k"].append(ctx_out(k, (kv_heads, hd)))
            outs["gqa_v"].append(ctx_out(v, (kv_heads, hd)))
            ck = cache_gqa_k[:, j].reshape(db * past, n_kv)
            cv = cache_gqa_v[:, j].reshape(db * past, n_kv)
            ctx_part, lat_parts = kv_parts(k16, v16, ck, cv, hd, hd)
            o = attention(q, ctx_part, lat_parts, heads=heads, kv_share=group, n_maps=1, dk=hd, dv=hd,
                          hpb_ctx=2 * group, hpb_lat=2 * group, tq_lat=tq_dense, name="attn_gqa")
            w_o = gqa_w_o[j]

        elif kind == 1:
            q_rank, kv_rank = mla_q_norm.shape[1], mla_kv_norm.shape[1]
            w_in = mla_w_in[j].astype(BF16)
            rope_dim = w_in.shape[1] - q_rank - kv_rank
            w_kpe = jnp.pad(w_in[:, q_rank + kv_rank:], ((0, 0), (0, LANES - rope_dim)))
            heads = mla_w_o.shape[1] // hd
            q_scale = (hd + rope_dim) ** -0.5 * LOG2_E
            cq = _matmul(h, w_in, BF16, w_cols=(0, q_rank), gain=vec(mla_q_norm[j]), norm_width=q_rank)
            ckv = _matmul(h, w_in, F32, w_cols=(q_rank, kv_rank), gain=vec(mla_kv_norm[j]), norm_width=kv_rank)
            kpe = _matmul(h, w_kpe, F32, rope_tabs=rope64, rope_half=rope_dim // 2)
            outs["mla_ckv"].append(ctx_out(ckv, (kv_rank,)))
            outs["mla_kpe"].append(ctx_out(kpe[:, :rope_dim], (rope_dim,)))
            w_q = jnp.pad(mla_w_q_up[j].astype(BF16).reshape(q_rank, heads, hd + rope_dim),
                          ((0, 0), (0, 0), (0, LANES - rope_dim))).reshape(q_rank, heads * 2 * hd)
            q = _matmul(cq, w_q, BF16, rope_tabs=rope64, rope_half=rope_dim // 2, rope_every=2, out_scale=q_scale)
            ckv_all = jnp.concatenate([ckv, cache_mla_ckv[:, j].reshape(db * past, kv_rank)], axis=0)
            kv = _matmul(ckv_all, mla_w_kv_up[j].astype(BF16), BF16)
            ckpe = jnp.pad(cache_mla_kpe[:, j].reshape(db * past, rope_dim), ((0, 0), (0, LANES - rope_dim)))
            cache_blk0 = t // past
            ctx_part = [(kv, seq, lambda b: b, 2 * hd), (kpe, seq, lambda b: b, None)]
            lat_parts = [[(kv, past, lambda b: cache_blk0 + b, 2 * hd), (ckpe, past, lambda b: b, None)],
                         [(kv, dseq, lambda b: lat_blk0 + b, 2 * hd), (kpe, dseq, lambda b: lat_blk0 + b, None)]]
            o = attention(q, ctx_part, lat_parts, heads=heads, kv_share=1, n_maps=1, dk=2 * hd, dv=hd,
                          hpb_ctx=8, hpb_lat=4, tq_lat=tq_dense, name="attn_mla", fill=_fill_mla)
            w_o = mla_w_o[j]

        elif kind == 2:
            lambda_init = 0.8 - 0.6 * math.exp(-0.3 * layer)
            w = diff_w_qkv[j].astype(BF16)
            n_w = w.shape[1] // 3
            heads = n_w // (2 * hd)
            q_scale = hd ** -0.5 * LOG2_E
            q = _matmul(h, w, BF16, w_cols=(0, n_w), rope_tabs=rope128, rope_half=hd // 2, out_scale=q_scale)
            k, k16 = _matmul(h, w, F32, w_cols=(n_w, n_w), rope_tabs=rope128, rope_half=hd // 2, bf16_copy=True)
            v, v16 = _matmul(h, w, F32, w_cols=(2 * n_w, n_w), bf16_copy=True)
            outs["diff_k"].append(ctx_out(k, (heads, 2, hd)))
            outs["diff_v"].append(ctx_out(v, (heads, 2 * hd)))
            ck = cache_diff_k[:, j].reshape(db * past, n_w)
            cv = cache_diff_v[:, j].reshape(db * past, n_w)
            lam = jnp.stack([diff_lam_q1[j], diff_lam_k1[j], diff_lam_q2[j], diff_lam_k2[j]]).astype(F32)
            ctx_part, lat_parts = kv_parts(k16, v16, ck, cv, 2 * hd, 2 * hd)
            o = attention(q, ctx_part, lat_parts, heads=heads, kv_share=1, n_maps=2, dk=hd, dv=2 * hd,
                          hpb_ctx=4, hpb_lat=2, tq_lat=tq_dense, name="attn_diff",
                          head_operands=[(lam, (4, hd)), (vec(diff_subln[j]), (1, 2 * hd))], lambda_init=lambda_init)
            w_o = diff_w_o[j]

        else:
            w = na_w_qkv[j].astype(BF16)
            n_w = w.shape[1] // 3
            heads = n_w // hd
            q_scale = hd ** -0.5 * LOG2_E
            q = _matmul(h, w, BF16, w_cols=(0, n_w), out_scale=q_scale)
            k, k16 = _matmul(h, w, F32, w_cols=(n_w, n_w), bf16_copy=True)
            v, v16 = _matmul(h, w, F32, w_cols=(2 * n_w, n_w), bf16_copy=True)
            outs["na_k"].append(ctx_out(k, (heads, hd)))
            outs["na_v"].append(ctx_out(v, (heads, hd)))
            ck = cache_na_k[:, j].reshape(db * past, n_w)
            cv = cache_na_v[:, j].reshape(db * past, n_w)
            n_rows = dseq // GRID_W
            tq_na = _pick(dseq, 256, GRID_W)
            band_rows, ridx, valid, thresholds = _na_band_plan(n_rows, tq_na)
            bias = _na_bias(na_rpb[j], ridx, valid)

            def case_of(i):
                return sum((i >= th).astype(jnp.int32) for th in thresholds) if thresholds else 0

            ctx_part, lat_parts = kv_parts(k16, v16, ck, cv, hd, hd)
            o = attention(q, ctx_part, lat_parts, heads=heads, kv_share=1, n_maps=1, dk=hd, dv=hd,
                          hpb_ctx=8, hpb_lat=8, tq_lat=tq_na, name="attn_na", bias=(bias, case_of),
                          band=(n_rows, band_rows))
            w_o = na_w_o[j]

        y = _matmul(o, w_o.astype(BF16), F32)
        o_buf[0] = o
        x, h = _rows(x, y, mods, tr, post=(vec(norm_mix_post[layer]), seg, 2),
                     pre=(vec(norm_ffn_pre[layer]), seg, 3, 4))

        act = _matmul(h, ffn_w_gu, BF16, layer=layer, swiglu=True, bm=2048, bn=256, single_buffer_a=True)
        y = _matmul(act, ffn_down_bf16, F32, layer=layer, bm=512, bn=512)
        post = (vec(norm_ffn_post[layer]), seg, 5)
        if layer + 1 < depth:
            x, h = _rows(x, y, mods, tr, post=post,
                         pre=(vec(norm_mix_pre[layer + 1]), seg_of_tile_at(layer + 1), 0, 1))
        else:
            (y_prompt,) = _rows(x, y, mods, tr, post=post, tile0=0, n_tiles=n_ctx_tiles)
            (y_sample,) = _rows(x, y, mods, tr, post=post, tile0=n_ctx_tiles, n_tiles=ts // tr)

    return (y_prompt.reshape(nb, seq, d), y_sample.reshape(db, dseq, d),
            jnp.stack(outs["gqa_k"], axis=1), jnp.stack(outs["gqa_v"], axis=1),
            jnp.stack(outs["mla_ckv"], axis=1), jnp.stack(outs["mla_kpe"], axis=1),
            jnp.stack(outs["diff_k"], axis=1), jnp.stack(outs["diff_v"], axis=1),
            jnp.stack(outs["na_k"], axis=1), jnp.stack(outs["na_v"], axis=1))
```

```python
import functools
import math

import numpy as np
import jax
import jax.numpy as jnp
from jax import lax
from jax.experimental import pallas as pl
from jax.experimental.pallas import tpu as pltpu

GRID_W = 64
N_MIXERS = 4
ROPE_BASE = 10000.0
NORM_EPS = 1e-6
NEG_INF = -1e30
HEAD_DIM = 128
B_ROPE_DIM = 64
WIN_R = 8
WIN_C = 16
LOG2_E = math.log2(math.e)

LANES = 128
MXU_COLS = 256
MOD_ROWS = 16
V7X_VMEM_BYTES = 64 * 1024 * 1024
VMEM_CAP_BYTES = V7X_VMEM_BYTES - 8 * 1024 * 1024

BF16 = jnp.bfloat16
F32 = jnp.float32


def _pick(n, target, mult):
    best = None
    for t in range(mult, min(n, target) + 1, mult):
        if n % t == 0:
            best = t
    return n if best is None else best


def _params(vmem_estimate_bytes, n_grid):
    limit = int(min(VMEM_CAP_BYTES, max(32 * 1024 * 1024, vmem_estimate_bytes)))
    return pltpu.CompilerParams(dimension_semantics=("arbitrary",) * n_grid, vmem_limit_bytes=limit)


def _adaln_kernel(c_ref, w_ref, b_ref, o_ref):
    c = c_ref[...]
    a = (c * jax.nn.sigmoid(c)).astype(BF16)
    o_ref[...] = jnp.dot(a, w_ref[...].astype(BF16), preferred_element_type=F32) + b_ref[...]


def _adaln(cond, ada_w, ada_b):
    depth, d, n = ada_w.shape
    tn = _pick(n, 512, LANES)
    est = 2 * d * tn * 4 + d * tn * 2 + 4 * MOD_ROWS * (d + tn) * 4
    return pl.pallas_call(
        _adaln_kernel,
        grid=(depth, n // tn),
        in_specs=[
            pl.BlockSpec((MOD_ROWS, d), lambda l, j: (0, 0)),
            pl.BlockSpec((None, d, tn), lambda l, j: (l, 0, j)),
            pl.BlockSpec((None, 1, tn), lambda l, j: (l, 0, j)),
        ],
        out_specs=pl.BlockSpec((None, MOD_ROWS, tn), lambda l, j: (l, 0, j)),
        out_shape=jax.ShapeDtypeStruct((depth, MOD_ROWS, n), F32),
        compiler_params=_params(est + (8 << 20), 2),
        name="adaln",
    )(cond, ada_w, ada_b.reshape(depth, 1, n))


def _rms(x, g):
    return x * lax.rsqrt(jnp.mean(x * x, axis=-1, keepdims=True) + NORM_EPS) * g


def _rows_kernel(*refs, has_post, has_pre, split_tiles):
    refs = list(refs)
    x = refs.pop(0)[...]
    if split_tiles is not None:
        x = jnp.where(pl.program_id(0) < split_tiles, x, refs.pop(0)[...])
    if has_post:
        y_ref, gpost_ref, gate_ref = refs[:3]
        refs = refs[3:]
        x = x + gate_ref[0] * _rms(y_ref[...], gpost_ref[...])
    if has_pre:
        gpre_ref, shift_ref, scale_ref = refs[:3]
        refs = refs[3:]
    if has_post:
        refs.pop(0)[...] = x
    if has_pre:
        refs.pop(0)[...] = (_rms(x, gpre_ref[...]) * (1.0 + scale_ref[0]) + shift_ref[0]).astype(BF16)


def _rows(x, y, mods, tr, post=None, pre=None, tile0=0, n_tiles=None):
    xs = x if isinstance(x, tuple) else (x,)
    t, d = sum(a.shape[0] for a in xs), xs[0].shape[1]
    row_spec = pl.BlockSpec((tr, d), lambda i: (i + tile0, 0))
    split_tiles = None
    args, in_specs = [x], [row_spec]
    if isinstance(x, tuple):
        assert tile0 == 0
        split_tiles = xs[0].shape[0] // tr
        args = list(xs)
        in_specs = [pl.BlockSpec((tr, d), lambda i: (jnp.minimum(i, split_tiles - 1), 0)),
                    pl.BlockSpec((tr, d), lambda i: (jnp.maximum(i - split_tiles, 0), 0))]
    n_tiles = t // tr if n_tiles is None else n_tiles
    out_spec = pl.BlockSpec((tr, d), lambda i: (i, 0))
    vec_spec = pl.BlockSpec((1, d), lambda i: (0, 0))

    def mod_spec(seg_of_tile, slot):
        return pl.BlockSpec((1, 1, d), lambda i: (seg_of_tile(i + tile0), 0, slot))

    out_shape, out_specs = [], []
    if post is not None:
        args += [y, post[0], mods]
        in_specs += [row_spec, vec_spec, mod_spec(post[1], post[2])]
        out_shape.append(jax.ShapeDtypeStruct((n_tiles * tr, d), F32))
        out_specs.append(out_spec)
    if pre is not None:
        args += [pre[0], mods, mods]
        in_specs += [vec_spec, mod_spec(pre[1], pre[2]), mod_spec(pre[1], pre[3])]
        out_shape.append(jax.ShapeDtypeStruct((n_tiles * tr, d), BF16))
        out_specs.append(out_spec)
    est = 2 * tr * d * 4 * (len(args) + 2) + (8 << 20)
    out = pl.pallas_call(
        functools.partial(_rows_kernel, has_post=post is not None, has_pre=pre is not None, split_tiles=split_tiles),
        grid=(n_tiles,),
        in_specs=in_specs,
        out_specs=out_specs,
        out_shape=out_shape,
        compiler_params=_params(est, 1),
        name="rows",
    )(*args)
    return out


def _rope_lanes(y, tabs, half):
    if half == LANES // 2:
        return y * tabs[0] + pltpu.roll(y, LANES // 2, 1) * tabs[1]
    return (y * tabs[0] + pltpu.roll(y, LANES - half, 1) * tabs[1] + pltpu.roll(y, half, 1) * tabs[2])


def _mm_kernel(*refs, nk, swiglu, norm_width, rope_half, rope_every, out_scale, sub_n, n_out):
    refs = list(refs)
    a_ref = refs.pop(0)
    w_ref = refs.pop(0)
    w2_ref = refs.pop(0) if swiglu else None
    gain_ref = refs.pop(0) if norm_width else None
    n_tab = 0 if not rope_half else (2 if rope_half == LANES // 2 else 3)
    tab_refs = [refs.pop(0) for _ in range(n_tab)]
    out_refs = [refs.pop(0) for _ in range(n_out)]
    o_ref = out_refs[0]
    acc_refs = refs

    a = a_ref[...].astype(BF16)
    bn = o_ref.shape[-1]

    def epilogue(vals, col0):
        y = vals[0]
        if swiglu:
            y = (y * jax.nn.sigmoid(y)) * vals[1]
        if norm_width or rope_half:
            tabs = [r[...] for r in tab_refs]
            width = norm_width if norm_width else LANES
            for c in range(y.shape[-1] // width):
                sl = slice(col0 + c * width, col0 + (c + 1) * width)
                yc = y[:, c * width:(c + 1) * width]
                if norm_width:
                    yc = _rms(yc, gain_ref[:, sl])
                if rope_half and c % rope_every == rope_every - 1:
                    yc = _rope_lanes(yc, tabs, rope_half)
                if out_scale != 1.0:
                    yc = yc * out_scale
                for r in out_refs:
                    r[:, sl] = yc.astype(r.dtype)
        else:
            if out_scale != 1.0:
                y = y * out_scale
            for r in out_refs:
                r[:, col0:col0 + y.shape[-1]] = y.astype(r.dtype)

    if nk == 1 and sub_n < bn:
        for c0 in range(0, bn, sub_n):
            epilogue([jnp.dot(a, w_ref[:, c0:c0 + sub_n].astype(BF16), preferred_element_type=F32)], c0)
        return

    parts = [jnp.dot(a, w_ref[...].astype(BF16), preferred_element_type=F32)]
    if swiglu:
        parts.append(jnp.dot(a, w2_ref[...].astype(BF16), preferred_element_type=F32))

    if nk == 1:
        epilogue(parts, 0)
    else:
        k = pl.program_id(2)

        @pl.when(k == 0)
        def _():
            for r, p in zip(acc_refs, parts):
                r[...] = p

        @pl.when(k > 0)
        def _():
            for r, p in zip(acc_refs, parts):
                r[...] += p

        @pl.when(k == nk - 1)
        def _():
            epilogue([r[...] for r in acc_refs], 0)


def _matmul(a, w, out_dtype, *, layer=0, w_cols=None, swiglu=False, gain=None, norm_width=0, rope_tabs=None,
            rope_half=0, rope_every=1, out_scale=1.0, bm=1024, bn=None, bk=None, single_buffer_a=False,
            bf16_copy=False):
    m, kdim = a.shape
    col0, n = (0, w.shape[-1]) if w_cols is None else w_cols
    if swiglu:
        col0, n = 0, w.shape[-1] // 2
    bm = _pick(m, bm, 8)
    if bn is None:
        bn = 512 if (swiglu or bk) else 1024
    bn = _pick(math.gcd(n, col0) if col0 else n, bn, LANES * rope_every)
    if norm_width > LANES:
        bn = norm_width
    bk = kdim if bk is None else _pick(kdim, bk, LANES)
    nk = kdim // bk
    jb = col0 // bn
    sub_n = bn
    if (norm_width or rope_half) and norm_width <= MXU_COLS and nk == 1:
        sub_n = _pick(bn, MXU_COLS, LANES * rope_every)

    def w_spec(col_blk0):
        if w.ndim == 3:
            return pl.BlockSpec((None, bk, bn), lambda i, j, k: (layer, k, j + col_blk0))
        return pl.BlockSpec((bk, bn), lambda i, j, k: (k, j + col_blk0))

    a_buffers = 1 if (single_buffer_a and nk == 1) else 2
    a_mode = dict(pipeline_mode=pl.Buffered(1)) if a_buffers == 1 else {}
    args = [a, w]
    in_specs = [pl.BlockSpec((bm, bk), lambda i, j, k: (i, k), **a_mode), w_spec(jb)]
    if swiglu:
        args.append(w)
        in_specs.append(w_spec(n // bn))
    if norm_width:
        args.append(gain)
        in_specs.append(pl.BlockSpec((1, bn), lambda i, j, k: (0, j)))
    if rope_half:
        for tab in rope_tabs:
            args.append(tab)
            in_specs.append(pl.BlockSpec((bm, LANES), lambda i, j, k: (i, 0)))
    n_w = 2 if swiglu else 1
    scratch = [pltpu.VMEM((bm, bn), F32) for _ in range(n_w)] if nk > 1 else []
    out_dtypes = [out_dtype] + ([BF16] if bf16_copy else [])
    est = (a_buffers * bm * bk * a.dtype.itemsize + (bm * bk * 2 if a.dtype != BF16 else 0)
           + 2 * n_w * bk * bn * w.dtype.itemsize + (n_w * bk * bn * 2 if w.dtype != BF16 else 0)
           + sum(2 * bm * bn * jnp.dtype(dt).itemsize for dt in out_dtypes) + (n_w + 2) * bm * bn * 4 + (4 << 20))
    out = pl.pallas_call(
        functools.partial(_mm_kernel, nk=nk, swiglu=swiglu, norm_width=norm_width, rope_half=rope_half,
                          rope_every=rope_every, out_scale=out_scale, sub_n=sub_n, n_out=len(out_dtypes)),
        grid=(m // bm, n // bn, nk),
        in_specs=in_specs,
        out_specs=[pl.BlockSpec((bm, bn), lambda i, j, k: (i, j)) for _ in out_dtypes],
        out_shape=[jax.ShapeDtypeStruct((m, n), dt) for dt in out_dtypes],
        scratch_shapes=scratch,
        compiler_params=_params(est, 3),
        name="matmul",
    )(*args)
    return tuple(out) if bf16_copy else out[0]


def _dot_nt(q, k):
    return lax.dot_general(q, k, (((1,), (1,)), ((), ())), preferred_element_type=F32)


def _softmax_parts(s_parts):
    m = s_parts[0].max(axis=-1, keepdims=True)
    for s in s_parts[1:]:
        m = jnp.maximum(m, s.max(axis=-1, keepdims=True))
    e_parts = [jnp.exp2(s - m) for s in s_parts]
    denom = e_parts[0].sum(axis=-1, keepdims=True)
    for e in e_parts[1:]:
        denom = denom + e.sum(axis=-1, keepdims=True)
    return e_parts, denom


def _fill_kv(part_refs, kbf, vbf, row0, n_rows):
    k_ref, v_ref = part_refs
    kbf[row0:row0 + n_rows, :] = k_ref[...].astype(BF16)
    vbf[row0:row0 + n_rows, :] = v_ref[...].astype(BF16)


def _fill_mla(part_refs, kbf, vbf, row0, n_rows):
    kv_ref, kpe_ref = part_refs
    d = HEAD_DIM
    kpe = kpe_ref[...].astype(BF16)
    for slot in range(kv_ref.shape[1] // (2 * d)):
        kbf[row0:row0 + n_rows, 2 * slot * d:(2 * slot + 1) * d] = kv_ref[:, 2 * slot * d:(2 * slot + 1) * d]
        kbf[row0:row0 + n_rows, (2 * slot + 1) * d:(2 * slot + 2) * d] = kpe
        vbf[row0:row0 + n_rows, slot * d:(slot + 1) * d] = kv_ref[:, (2 * slot + 1) * d:(2 * slot + 2) * d]


def _attn_kernel(*refs, hpb, kv_share, n_maps, dk, dv, n_parts, refs_per_part, fill, band, lambda_init):
    refs = list(refs)
    if n_maps == 2:
        lam_ref, gain_ref = refs.pop(0), refs.pop(0)
    q_ref = refs.pop(0)
    part_refs = [tuple(refs.pop(0) for _ in range(refs_per_part)) for _ in range(n_parts)]
    bias_ref = refs.pop(0) if band else None
    o_ref = refs.pop(0)
    kbf, vbf = refs
    tq = q_ref.shape[0]
    part_rows = [p[0].shape[0] for p in part_refs]

    @pl.when(pl.program_id(2) == 0)
    def _():
        row0 = 0
        for p, n in zip(part_refs, part_rows):
            fill(p, kbf, vbf, row0, n)
            row0 += n

    if band:
        n_rows, band_rows = band
        wr = min(WIN_R, n_rows)
        r0 = pl.program_id(2) * (tq // GRID_W)
        assert part_rows[0] % GRID_W == 0
        start = part_rows[0] + jnp.clip(r0 - wr // 2, 0, n_rows - band_rows) * GRID_W
        key_rows = [pl.ds(0, part_rows[0]), pl.ds(pl.multiple_of(start, GRID_W), band_rows * GRID_W)]
    else:
        key_rows = [pl.ds(0, sum(part_rows))]

    if n_maps == 2:
        lam = (jnp.exp(jnp.sum(lam_ref[0:1, :] * lam_ref[1:2, :], axis=-1, keepdims=True))
               - jnp.exp(jnp.sum(lam_ref[2:3, :] * lam_ref[3:4, :], axis=-1, keepdims=True)) + lambda_init)

    for hs in range(hpb):
        slot = hs // kv_share
        maps = []
        for mp in range(n_maps):
            q = q_ref[:, (hs * n_maps + mp) * dk:(hs * n_maps + mp + 1) * dk]
            kcols = slice((slot * n_maps + mp) * dk, (slot * n_maps + mp + 1) * dk)
            s_parts = [_dot_nt(q, kbf[rows, kcols]) for rows in key_rows]
            if band:
                s_parts[-1] = s_parts[-1] + bias_ref[hs]
            maps.append(_softmax_parts(s_parts))
        vcols = slice(slot * dv, (slot + 1) * dv)
        if n_maps == 1:
            (e_parts, denom), = maps
            wgts = [e.astype(BF16) for e in e_parts]
        else:
            (e0, l0), (e1, l1) = maps
            ratio = lam * l0 / l1
            wgts = [(a - b * ratio).astype(BF16) for a, b in zip(e0, e1)]
            denom = l0
        o = jnp.dot(wgts[0], vbf[key_rows[0], vcols], preferred_element_type=F32)
        for wgt, rows in zip(wgts[1:], key_rows[1:]):
            o = o + jnp.dot(wgt, vbf[rows, vcols], preferred_element_type=F32)
        o = o / denom
        if n_maps == 2:
            o = _rms(o, gain_ref[...]) * (1.0 - lambda_init)
        o_ref[:, hs * dv:(hs + 1) * dv] = o.astype(o_ref.dtype)


def _drop_ref(kernel_fn, index):
    def wrapped(*refs):
        return kernel_fn(*refs[:index], *refs[index + 1:])
    return wrapped


def _attention_call(grid, operands, *, t_rows, heads, tq, q_tile_of, key_rows, cfg, name, into=None):
    hpb, dv, n_maps, dk = cfg["hpb"], cfg["dv"], cfg["n_maps"], cfg["dk"]
    slots = hpb // cfg["kv_share"]
    in_specs = [pl.BlockSpec(blk, imap) for _, blk, imap in operands]
    args = [arr for arr, _, _ in operands]
    kernel_fn = functools.partial(_attn_kernel, **cfg)
    aliases = {}
    if into is not None:
        kernel_fn = _drop_ref(kernel_fn, len(args))
        aliases = {len(args): 0}
        in_specs.append(pl.BlockSpec(memory_space=pl.ANY))
        args.append(into)
    est = (hpb * n_maps * tq * key_rows * 10 + 3 * key_rows * slots * (n_maps * dk + dv) * 4
           + 4 * tq * hpb * (n_maps * dk + dv) * 2 + (8 << 20))
    return pl.pallas_call(
        kernel_fn,
        grid=grid,
        in_specs=in_specs,
        out_specs=pl.BlockSpec((tq, hpb * dv), lambda b, g, i: (q_tile_of(b, i), g)),
        out_shape=jax.ShapeDtypeStruct((t_rows, heads * dv), BF16),
        scratch_shapes=[pltpu.VMEM((key_rows, slots * n_maps * dk), BF16),
                        pltpu.VMEM((key_rows, slots * dv), BF16)],
        input_output_aliases=aliases,
        compiler_params=_params(est, 3),
        name=name,
    )(*args)


def _rope_tables(n_ctx_rows, dec_batch, dec_seq, rot_dim):
    t = jnp.arange(dec_seq)
    per_axis = rot_dim // 2
    inv_freq = ROPE_BASE ** (-jnp.arange(0, per_axis, 2, dtype=F32) / per_axis)
    row = (t // GRID_W).astype(F32)[:, None] * inv_freq
    col = (t % GRID_W).astype(F32)[:, None] * inv_freq
    ang = jnp.concatenate([row, col], axis=-1)
    cos, sin = jnp.cos(ang), jnp.sin(ang)
    pad = LANES - rot_dim
    ones = jnp.ones((dec_seq, pad), F32)
    zeros_half = jnp.zeros_like(sin)
    zeros_pad = jnp.zeros((dec_seq, pad), F32)
    if pad == 0:
        tabs = [jnp.concatenate([cos, cos], -1), jnp.concatenate([-sin, sin], -1)]
    else:
        tabs = [jnp.concatenate([cos, cos, ones], -1),
                jnp.concatenate([-sin, zeros_half, zeros_pad], -1),
                jnp.concatenate([zeros_half, sin, zeros_pad], -1)]
    out = []
    for i, tab in enumerate(tabs):
        ctx = jnp.ones((n_ctx_rows, LANES), F32) if i == 0 else jnp.zeros((n_ctx_rows, LANES), F32)
        out.append(jnp.concatenate([ctx, jnp.tile(tab, (dec_batch, 1))], axis=0))
    return out


def _na_band_plan(n_rows, tq):
    wr = min(WIN_R, n_rows)
    rt = tq // GRID_W
    rows_per_lane_tile = LANES // GRID_W
    band_rows = min(n_rows, -(-(wr + rt - 1) // rows_per_lane_tile) * rows_per_lane_tile)
    cases, case_of_tile = [], []
    for qt in range(n_rows // rt):
        r0 = qt * rt
        bs = int(np.clip(r0 - wr // 2, 0, n_rows - band_rows))
        r = r0 + np.arange(rt)[:, None]
        kr = bs + np.arange(band_rows)[None, :]
        rs = np.clip(r - wr // 2, 0, n_rows - wr)
        valid = (kr >= rs) & (kr < rs + wr)
        ridx = np.clip(kr - r + WIN_R - 1, 0, 2 * WIN_R - 2)
        key = (valid.tobytes(), ridx.tobytes())
        for ci, (k0, _, _) in enumerate(cases):
            if k0 == key:
                break
        else:
            ci = len(cases)
            cases.append((key, ridx, valid))
        case_of_tile.append(ci)
    if any(b < a for a, b in zip(case_of_tile, case_of_tile[1:])) or \
            sorted(set(case_of_tile)) != list(range(len(cases))):
        raise NotImplementedError("neighbourhood tile geometries are not monotone in the tile index")
    thresholds = [case_of_tile.index(ci) for ci in range(1, len(cases))]
    ridx = np.stack([c[1] for c in cases])
    valid = np.stack([c[2] for c in cases])
    return band_rows, ridx, valid, thresholds


def _na_bias(rpb, ridx, valid):
    col = jnp.arange(GRID_W)
    col_start = jnp.clip(col - WIN_C // 2, 0, GRID_W - WIN_C)
    in_win = (col[None, :] >= col_start[:, None]) & (col[None, :] < col_start[:, None] + WIN_C)
    col_off = jnp.clip(col[None, :] - col[:, None] + WIN_C - 1, 0, 2 * WIN_C - 2)
    col_bias = jnp.where(in_win[None, None], rpb[:, :, col_off].astype(F32) * LOG2_E, NEG_INF)
    masked = jnp.full_like(col_bias[:, :1], NEG_INF)
    col_bias_q = jnp.transpose(jnp.concatenate([col_bias, masked], axis=1), (0, 2, 1, 3))
    n_case, rt, band_rows = ridx.shape
    sel = np.where(valid, ridx, 2 * WIN_R - 1)
    heads = rpb.shape[0]
    cases = []
    for ci in range(n_case):
        rows = [jnp.take(col_bias_q, jnp.asarray(sel[ci, qr]), axis=2).reshape(heads, GRID_W, band_rows * GRID_W)
                for qr in range(rt)]
        cases.append(jnp.concatenate(rows, axis=1))
    return jnp.stack(cases, axis=1)


def kernel(x_prompt, x_sample, cache_gqa_k, cache_gqa_v, cache_mla_ckv, cache_mla_kpe, cache_diff_k, cache_diff_v, cache_na_k, cache_na_v, c, c_ctx, ada_w, ada_b, norm_mix_pre, norm_mix_post, norm_ffn_pre, norm_ffn_post, ffn_w_gu, ffn_w_down, gqa_w_qkv, gqa_q_norm, gqa_k_norm, gqa_w_o, mla_w_in, mla_q_norm, mla_kv_norm, mla_w_q_up, mla_w_kv_up, mla_w_o, diff_w_qkv, diff_lam_q1, diff_lam_k1, diff_lam_q2, diff_lam_k2, diff_subln, diff_w_o, na_w_qkv, na_rpb, na_w_o):
    nb, seq, d = x_prompt.shape
    db, dseq, _ = x_sample.shape
    past = cache_gqa_k.shape[2]
    depth = ada_w.shape[0]
    tp, ts = nb * seq, db * dseq
    t = tp + ts
    hd = HEAD_DIM
    assert db + 1 <= MOD_ROWS and dseq % GRID_W == 0
    assert tp % dseq == 0 and t % past == 0 and tp % past == 0

    tr = _pick(math.gcd(seq, dseq), 256, 8)
    n_ctx_tiles, tiles_per_dec = tp // tr, dseq // tr

    def seg_of_tile_at(layer):
        return lambda i: layer * MOD_ROWS + jnp.where(i < n_ctx_tiles, 0, 1 + (i - n_ctx_tiles) // tiles_per_dec)

    cond = jnp.concatenate([c_ctx[None, :], c, jnp.zeros((MOD_ROWS - 1 - db, d), F32)], axis=0)
    mods = _adaln(cond, ada_w, ada_b).reshape(depth * MOD_ROWS, 1, 6 * d)

    x = (x_prompt.reshape(tp, d), x_sample.reshape(ts, d))
    o_buf = [None]

    rope128 = _rope_tables(tp, db, dseq, hd)
    rope64 = _rope_tables(tp, db, dseq, B_ROPE_DIM)
    lat_blk0 = tp // dseq
    tq_dense = _pick(dseq, 512, GRID_W)

    def vec(v):
        return v.reshape(1, -1).astype(F32)

    def ctx_out(arr, shape):
        return arr[:tp].reshape((nb, seq) + shape)

    def attention(q, ctx_part, lat_parts, *, heads, kv_share, n_maps, dk, dv, hpb_ctx, hpb_lat, tq_lat, name,
                  fill=_fill_kv, head_operands=(), bias=None, band=None, lambda_init=0.0):
        def part_operands(part, hpb):
            slots = hpb // kv_share
            ops = []
            for arr, rows, blk_of, cols in part:
                if cols is None:
                    ops.append((arr, (rows, LANES), lambda b, g, i, blk_of=blk_of: (blk_of(b), 0)))
                else:
                    ops.append((arr, (rows, slots * cols), lambda b, g, i, blk_of=blk_of: (blk_of(b), g)))
            return ops

        def cfg(hpb, n_parts, band_cfg):
            return dict(hpb=hpb, kv_share=kv_share, n_maps=n_maps, dk=dk, dv=dv, n_parts=n_parts,
                        refs_per_part=len(ctx_part), fill=fill, band=band_cfg, lambda_init=lambda_init)

        qw = n_maps * dk
        hpb_c = min(hpb_ctx, heads)
        ops = [(a, blk, (lambda b, g, i: (0, 0))) for a, blk in head_operands]
        ops.append((q, (seq, hpb_c * qw), lambda b, g, i: (b, g)))
        ops += part_operands(ctx_part, hpb_c)
        buf = o_buf[0]
        if buf is None or buf.shape != (t, heads * dv):
            buf = jnp.zeros((t, heads * dv), BF16)
        o = _attention_call((nb, heads // hpb_c, 1), ops, t_rows=t, heads=heads, tq=seq,
                            q_tile_of=lambda b, i: b, key_rows=seq, cfg=cfg(hpb_c, 1, None), name=name + "_ctx",
                            into=buf)
        hpb_l = min(hpb_lat, heads)
        n_qt, ctx_tiles = dseq // tq_lat, tp // tq_lat

        def q_tile_of(b, i):
            return ctx_tiles + b * n_qt + i

        ops = [(a, blk, (lambda b, g, i: (0, 0))) for a, blk in head_operands]
        ops.append((q, (tq_lat, hpb_l * qw), lambda b, g, i: (q_tile_of(b, i), g)))
        for part in lat_parts:
            ops += part_operands(part, hpb_l)
        if bias is not None:
            bias_arr, case_of = bias
            ops.append((bias_arr, (hpb_l, None) + bias_arr.shape[2:], lambda b, g, i: (g, case_of(i), 0, 0)))
        return _attention_call((db, heads // hpb_l, n_qt), ops, t_rows=t, heads=heads, tq=tq_lat,
                               q_tile_of=q_tile_of, key_rows=past + dseq, cfg=cfg(hpb_l, len(lat_parts), band),
                               name=name + "_lat", into=o)

    def kv_parts(k, v, ck, cv, cols_k, cols_v):
        ctx = [(k, seq, lambda b: b, cols_k), (v, seq, lambda b: b, cols_v)]
        lat = [[(ck, past, lambda b: b, cols_k), (cv, past, lambda b: b, cols_v)],
               [(k, dseq, lambda b: lat_blk0 + b, cols_k), (v, dseq, lambda b: lat_blk0 + b, cols_v)]]
        return ctx, lat

    outs = {k: [] for k in ("gqa_k", "gqa_v", "mla_ckv", "mla_kpe", "diff_k", "diff_v", "na_k", "na_v")}

    ffn_down_bf16 = ffn_w_down.astype(BF16)

    (h,) = _rows(x, None, mods, tr, pre=(vec(norm_mix_pre[0]), seg_of_tile_at(0), 0, 1))

    for layer in range(depth):
        kind, j = layer % N_MIXERS, layer // N_MIXERS
        seg = seg_of_tile_at(layer)

        if kind == 0:
            w = gqa_w_qkv[j].astype(BF16)
            n_q = gqa_w_o.shape[1]
            n_kv = (w.shape[1] - n_q) // 2
            heads, kv_heads = n_q // hd, n_kv // hd
            group = heads // kv_heads
            q_scale = hd ** -0.5 * LOG2_E
            q = _matmul(h, w, BF16, w_cols=(0, n_q), gain=jnp.tile(vec(gqa_q_norm[j]), (1, heads)), norm_width=hd,
                        rope_tabs=rope128, rope_half=hd // 2, out_scale=q_scale)
            k, k16 = _matmul(h, w, F32, w_cols=(n_q, n_kv), gain=jnp.tile(vec(gqa_k_norm[j]), (1, kv_heads)),
                             norm_width=hd, rope_tabs=rope128, rope_half=hd // 2, bf16_copy=True)
            v, v16 = _matmul(h, w, F32, w_cols=(n_q + n_kv, n_kv), bf16_copy=True)
            outs["gqa_k"].append(ctx_out(k, (kv_heads, hd)))
            outs["gqa_v"].append(ctx_out(v, (kv_heads, hd)))
            ck = cache_gqa_k[:, j].reshape(db * past, n_kv)
            cv = cache_gqa_v[:, j].reshape(db * past, n_kv)
            ctx_part, lat_parts = kv_parts(k16, v16, ck, cv, hd, hd)
            o = attention(q, ctx_part, lat_parts, heads=heads, kv_share=group, n_maps=1, dk=hd, dv=hd,
                          hpb_ctx=4 * group, hpb_lat=2 * group, tq_lat=tq_dense, name="attn_gqa")
            w_o = gqa_w_o[j]

        elif kind == 1:
            q_rank, kv_rank = mla_q_norm.shape[1], mla_kv_norm.shape[1]
            w_in = mla_w_in[j].astype(BF16)
            rope_dim = w_in.shape[1] - q_rank - kv_rank
            w_kpe = jnp.pad(w_in[:, q_rank + kv_rank:], ((0, 0), (0, LANES - rope_dim)))
            heads = mla_w_o.shape[1] // hd
            q_scale = (hd + rope_dim) ** -0.5 * LOG2_E
            cq = _matmul(h, w_in, BF16, w_cols=(0, q_rank), gain=vec(mla_q_norm[j]), norm_width=q_rank)
            ckv = _matmul(h, w_in, F32, w_cols=(q_rank, kv_rank), gain=vec(mla_kv_norm[j]), norm_width=kv_rank)
            kpe = _matmul(h, w_kpe, F32, rope_tabs=rope64, rope_half=rope_dim // 2)
            outs["mla_ckv"].append(ctx_out(ckv, (kv_rank,)))
            outs["mla_kpe"].append(ctx_out(kpe[:, :rope_dim], (rope_dim,)))
            w_q = jnp.pad(mla_w_q_up[j].astype(BF16).reshape(q_rank, heads, hd + rope_dim),
                          ((0, 0), (0, 0), (0, LANES - rope_dim))).reshape(q_rank, heads * 2 * hd)
            q = _matmul(cq, w_q, BF16, rope_tabs=rope64, rope_half=rope_dim // 2, rope_every=2, out_scale=q_scale)
            ckv_all = jnp.concatenate([ckv, cache_mla_ckv[:, j].reshape(db * past, kv_rank)], axis=0)
            kv = _matmul(ckv_all, mla_w_kv_up[j].astype(BF16), BF16)
            ckpe = jnp.pad(cache_mla_kpe[:, j].reshape(db * past, rope_dim), ((0, 0), (0, LANES - rope_dim)))
            cache_blk0 = t // past
            ctx_part = [(kv, seq, lambda b: b, 2 * hd), (kpe, seq, lambda b: b, None)]
            lat_parts = [[(kv, past, lambda b: cache_blk0 + b, 2 * hd), (ckpe, past, lambda b: b, None)],
                         [(kv, dseq, lambda b: lat_blk0 + b, 2 * hd), (kpe, dseq, lambda b: lat_blk0 + b, None)]]
            o = attention(q, ctx_part, lat_parts, heads=heads, kv_share=1, n_maps=1, dk=2 * hd, dv=hd,
                          hpb_ctx=16, hpb_lat=4, tq_lat=tq_dense, name="attn_mla", fill=_fill_mla)
            w_o = mla_w_o[j]

        elif kind == 2:
            lambda_init = 0.8 - 0.6 * math.exp(-0.3 * layer)
            w = diff_w_qkv[j].astype(BF16)
            n_w = w.shape[1] // 3
            heads = n_w // (2 * hd)
            q_scale = hd ** -0.5 * LOG2_E
            q = _matmul(h, w, BF16, w_cols=(0, n_w), rope_tabs=rope128, rope_half=hd // 2, out_scale=q_scale)
            k, k16 = _matmul(h, w, F32, w_cols=(n_w, n_w), rope_tabs=rope128, rope_half=hd // 2, bf16_copy=True)
            v, v16 = _matmul(h, w, F32, w_cols=(2 * n_w, n_w), bf16_copy=True)
            outs["diff_k"].append(ctx_out(k, (heads, 2, hd)))
            outs["diff_v"].append(ctx_out(v, (heads, 2 * hd)))
            ck = cache_diff_k[:, j].reshape(db * past, n_w)
            cv = cache_diff_v[:, j].reshape(db * past, n_w)
            lam = jnp.stack([diff_lam_q1[j], diff_lam_k1[j], diff_lam_q2[j], diff_lam_k2[j]]).astype(F32)
            ctx_part, lat_parts = kv_parts(k16, v16, ck, cv, 2 * hd, 2 * hd)
            o = attention(q, ctx_part, lat_parts, heads=heads, kv_share=1, n_maps=2, dk=hd, dv=2 * hd,
                          hpb_ctx=8, hpb_lat=4, tq_lat=tq_dense, name="attn_diff",
                          head_operands=[(lam, (4, hd)), (vec(diff_subln[j]), (1, 2 * hd))], lambda_init=lambda_init)
            w_o = diff_w_o[j]

        else:
            w = na_w_qkv[j].astype(BF16)
            n_w = w.shape[1] // 3
            heads = n_w // hd
            q_scale = hd ** -0.5 * LOG2_E
            q = _matmul(h, w, BF16, w_cols=(0, n_w), out_scale=q_scale)
            k, k16 = _matmul(h, w, F32, w_cols=(n_w, n_w), bf16_copy=True)
            v, v16 = _matmul(h, w, F32, w_cols=(2 * n_w, n_w), bf16_copy=True)
            outs["na_k"].append(ctx_out(k, (heads, hd)))
            outs["na_v"].append(ctx_out(v, (heads, hd)))
            ck = cache_na_k[:, j].reshape(db * past, n_w)
            cv = cache_na_v[:, j].reshape(db * past, n_w)
            n_rows = dseq // GRID_W
            tq_na = _pick(dseq, 256, GRID_W)
            band_rows, ridx, valid, thresholds = _na_band_plan(n_rows, tq_na)
            bias = _na_bias(na_rpb[j], ridx, valid)

            def case_of(i):
                return sum((i >= th).astype(jnp.int32) for th in thresholds) if thresholds else 0

            ctx_part, lat_parts = kv_parts(k16, v16, ck, cv, hd, hd)
            o = attention(q, ctx_part, lat_parts, heads=heads, kv_share=1, n_maps=1, dk=hd, dv=hd,
                          hpb_ctx=16, hpb_lat=8, tq_lat=tq_na, name="attn_na", bias=(bias, case_of),
                          band=(n_rows, band_rows))
            w_o = na_w_o[j]

        y = _matmul(o, w_o.astype(BF16), F32)
        o_buf[0] = o
        x, h = _rows(x, y, mods, tr, post=(vec(norm_mix_post[layer]), seg, 2),
                     pre=(vec(norm_ffn_pre[layer]), seg, 3, 4))

        act = _matmul(h, ffn_w_gu, BF16, layer=layer, swiglu=True, bm=2048, bn=256, single_buffer_a=True)
        y = _matmul(act, ffn_down_bf16, F32, layer=layer, bm=512, bn=512)
        post = (vec(norm_ffn_post[layer]), seg, 5)
        if layer + 1 < depth:
            x, h = _rows(x, y, mods, tr, post=post,
                         pre=(vec(norm_mix_pre[layer + 1]), seg_of_tile_at(layer + 1), 0, 1))
        else:
            (y_prompt,) = _rows(x, y, mods, tr, post=post, tile0=0, n_tiles=n_ctx_tiles)
            (y_sample,) = _rows(x, y, mods, tr, post=post, tile0=n_ctx_tiles, n_tiles=ts // tr)

    return (y_prompt.reshape(nb, seq, d), y_sample.reshape(db, dseq, d),
            jnp.stack(outs["gqa_k"], axis=1), jnp.stack(outs["gqa_v"], axis=1),
            jnp.stack(outs["mla_ckv"], axis=1), jnp.stack(outs["mla_kpe"], axis=1),
            jnp.stack(outs["diff_k"], axis=1), jnp.stack(outs["diff_v"], axis=1),
            jnp.stack(outs["na_k"], axis=1), jnp.stack(outs["na_v"], axis=1))
```

```python
import functools
import math

import numpy as np
import jax
import jax.numpy as jnp
from jax import lax
from jax.experimental import pallas as pl
from jax.experimental.pallas import tpu as pltpu

GRID_W = 64
N_MIXERS = 4
ROPE_BASE = 10000.0
NORM_EPS = 1e-6
NEG_INF = -1e30
HEAD_DIM = 128
B_ROPE_DIM = 64
WIN_R = 8
WIN_C = 16
LOG2_E = math.log2(math.e)

LANES = 128
MXU_COLS = 256
MOD_ROWS = 16
V7X_VMEM_BYTES = 64 * 1024 * 1024
VMEM_CAP_BYTES = V7X_VMEM_BYTES - 8 * 1024 * 1024

BF16 = jnp.bfloat16
F32 = jnp.float32


def _pick(n, target, mult):
    best = None
    for t in range(mult, min(n, target) + 1, mult):
        if n % t == 0:
            best = t
    return n if best is None else best


def _params(vmem_estimate_bytes, n_grid):
    limit = int(min(VMEM_CAP_BYTES, max(32 * 1024 * 1024, vmem_estimate_bytes)))
    return pltpu.CompilerParams(dimension_semantics=("arbitrary",) * n_grid, vmem_limit_bytes=limit)


def _adaln_kernel(c_ref, w_ref, b_ref, o_ref):
    c = c_ref[...]
    a = (c * jax.nn.sigmoid(c)).astype(BF16)
    o_ref[...] = jnp.dot(a, w_ref[...].astype(BF16), preferred_element_type=F32) + b_ref[...]


def _adaln(cond, ada_w, ada_b):
    depth, d, n = ada_w.shape
    tn = _pick(n, 512, LANES)
    est = 2 * d * tn * 4 + d * tn * 2 + 4 * MOD_ROWS * (d + tn) * 4
    return pl.pallas_call(
        _adaln_kernel,
        grid=(depth, n // tn),
        in_specs=[
            pl.BlockSpec((MOD_ROWS, d), lambda l, j: (0, 0)),
            pl.BlockSpec((None, d, tn), lambda l, j: (l, 0, j)),
            pl.BlockSpec((None, 1, tn), lambda l, j: (l, 0, j)),
        ],
        out_specs=pl.BlockSpec((None, MOD_ROWS, tn), lambda l, j: (l, 0, j)),
        out_shape=jax.ShapeDtypeStruct((depth, MOD_ROWS, n), F32),
        compiler_params=_params(est + (8 << 20), 2),
        name="adaln",
    )(cond, ada_w, ada_b.reshape(depth, 1, n))


def _rms(x, g):
    return x * lax.rsqrt(jnp.mean(x * x, axis=-1, keepdims=True) + NORM_EPS) * g


def _rows_kernel(*refs, has_post, has_pre, split_tiles):
    refs = list(refs)
    x = refs.pop(0)[...]
    if split_tiles is not None:
        x = jnp.where(pl.program_id(0) < split_tiles, x, refs.pop(0)[...])
    if has_post:
        y_ref, gpost_ref, gate_ref = refs[:3]
        refs = refs[3:]
        x = x + gate_ref[0] * _rms(y_ref[...], gpost_ref[...])
    if has_pre:
        gpre_ref, shift_ref, scale_ref = refs[:3]
        refs = refs[3:]
    if has_post:
        refs.pop(0)[...] = x
    if has_pre:
        refs.pop(0)[...] = (_rms(x, gpre_ref[...]) * (1.0 + scale_ref[0]) + shift_ref[0]).astype(BF16)


def _rows(x, y, mods, tr, post=None, pre=None, tile0=0, n_tiles=None):
    xs = x if isinstance(x, tuple) else (x,)
    t, d = sum(a.shape[0] for a in xs), xs[0].shape[1]
    row_spec = pl.BlockSpec((tr, d), lambda i: (i + tile0, 0))
    split_tiles = None
    args, in_specs = [x], [row_spec]
    if isinstance(x, tuple):
        assert tile0 == 0
        split_tiles = xs[0].shape[0] // tr
        args = list(xs)
        in_specs = [pl.BlockSpec((tr, d), lambda i: (jnp.minimum(i, split_tiles - 1), 0)),
                    pl.BlockSpec((tr, d), lambda i: (jnp.maximum(i - split_tiles, 0), 0))]
    n_tiles = t // tr if n_tiles is None else n_tiles
    out_spec = pl.BlockSpec((tr, d), lambda i: (i, 0))
    vec_spec = pl.BlockSpec((1, d), lambda i: (0, 0))

    def mod_spec(seg_of_tile, slot):
        return pl.BlockSpec((1, 1, d), lambda i: (seg_of_tile(i + tile0), 0, slot))

    out_shape, out_specs = [], []
    if post is not None:
        args += [y, post[0], mods]
        in_specs += [row_spec, vec_spec, mod_spec(post[1], post[2])]
        out_shape.append(jax.ShapeDtypeStruct((n_tiles * tr, d), F32))
        out_specs.append(out_spec)
    if pre is not None:
        args += [pre[0], mods, mods]
        in_specs += [vec_spec, mod_spec(pre[1], pre[2]), mod_spec(pre[1], pre[3])]
        out_shape.append(jax.ShapeDtypeStruct((n_tiles * tr, d), BF16))
        out_specs.append(out_spec)
    est = 2 * tr * d * 4 * (len(args) + 2) + (8 << 20)
    out = pl.pallas_call(
        functools.partial(_rows_kernel, has_post=post is not None, has_pre=pre is not None, split_tiles=split_tiles),
        grid=(n_tiles,),
        in_specs=in_specs,
        out_specs=out_specs,
        out_shape=out_shape,
        compiler_params=_params(est, 1),
        name="rows",
    )(*args)
    return out


def _rope_lanes(y, tabs, half):
    if half == LANES // 2:
        return y * tabs[0] + pltpu.roll(y, LANES // 2, 1) * tabs[1]
    return (y * tabs[0] + pltpu.roll(y, LANES - half, 1) * tabs[1] + pltpu.roll(y, half, 1) * tabs[2])


def _mm_kernel(*refs, nk, swiglu, norm_width, rope_half, rope_every, out_scale, sub_n):
    refs = list(refs)
    a_ref = refs.pop(0)
    w_ref = refs.pop(0)
    w2_ref = refs.pop(0) if swiglu else None
    gain_ref = refs.pop(0) if norm_width else None
    n_tab = 0 if not rope_half else (2 if rope_half == LANES // 2 else 3)
    tab_refs = [refs.pop(0) for _ in range(n_tab)]
    o_ref = refs.pop(0)
    acc_refs = refs

    a = a_ref[...].astype(BF16)
    bn = o_ref.shape[-1]

    def epilogue(vals, col0):
        y = vals[0]
        if swiglu:
            y = (y * jax.nn.sigmoid(y)) * vals[1]
        if norm_width or rope_half:
            tabs = [r[...] for r in tab_refs]
            width = norm_width if norm_width else LANES
            for c in range(y.shape[-1] // width):
                sl = slice(col0 + c * width, col0 + (c + 1) * width)
                yc = y[:, c * width:(c + 1) * width]
                if norm_width:
                    yc = _rms(yc, gain_ref[:, sl])
                if rope_half and c % rope_every == rope_every - 1:
                    yc = _rope_lanes(yc, tabs, rope_half)
                if out_scale != 1.0:
                    yc = yc * out_scale
                o_ref[:, sl] = yc.astype(o_ref.dtype)
        else:
            if out_scale != 1.0:
                y = y * out_scale
            o_ref[:, col0:col0 + y.shape[-1]] = y.astype(o_ref.dtype)

    if nk == 1 and sub_n < bn:
        for c0 in range(0, bn, sub_n):
            epilogue([jnp.dot(a, w_ref[:, c0:c0 + sub_n].astype(BF16), preferred_element_type=F32)], c0)
        return

    parts = [jnp.dot(a, w_ref[...].astype(BF16), preferred_element_type=F32)]
    if swiglu:
        parts.append(jnp.dot(a, w2_ref[...].astype(BF16), preferred_element_type=F32))

    if nk == 1:
        epilogue(parts, 0)
    else:
        k = pl.program_id(2)

        @pl.when(k == 0)
        def _():
            for r, p in zip(acc_refs, parts):
                r[...] = p

        @pl.when(k > 0)
        def _():
            for r, p in zip(acc_refs, parts):
                r[...] += p

        @pl.when(k == nk - 1)
        def _():
            epilogue([r[...] for r in acc_refs], 0)


def _matmul(a, w, out_dtype, *, layer=0, w_cols=None, swiglu=False, gain=None, norm_width=0, rope_tabs=None,
            rope_half=0, rope_every=1, out_scale=1.0, bm=1024, bn=None, bk=None, single_buffer_a=False,
            row_blocks=None):
    m, kdim = a.shape
    col0, n = (0, w.shape[-1]) if w_cols is None else w_cols
    if swiglu:
        col0, n = 0, w.shape[-1] // 2
    bm = _pick(m, bm, 8)
    i0, n_i = (0, m // bm) if row_blocks is None else row_blocks
    if bn is None:
        bn = 512 if (swiglu or bk) else 1024
    bn = _pick(math.gcd(n, col0) if col0 else n, bn, LANES * rope_every)
    if norm_width > LANES:
        bn = norm_width
    bk = kdim if bk is None else _pick(kdim, bk, LANES)
    nk = kdim // bk
    jb = col0 // bn
    sub_n = bn
    if (norm_width or rope_half) and norm_width <= MXU_COLS and nk == 1:
        sub_n = _pick(bn, MXU_COLS, LANES * rope_every)

    def w_spec(col_blk0):
        if w.ndim == 3:
            return pl.BlockSpec((None, bk, bn), lambda i, j, k: (layer, k, j + col_blk0))
        return pl.BlockSpec((bk, bn), lambda i, j, k: (k, j + col_blk0))

    a_buffers = 1 if (single_buffer_a and nk == 1) else 2
    a_mode = dict(pipeline_mode=pl.Buffered(1)) if a_buffers == 1 else {}
    args = [a, w]
    in_specs = [pl.BlockSpec((bm, bk), lambda i, j, k: (i + i0, k), **a_mode), w_spec(jb)]
    if swiglu:
        args.append(w)
        in_specs.append(w_spec(n // bn))
    if norm_width:
        args.append(gain)
        in_specs.append(pl.BlockSpec((1, bn), lambda i, j, k: (0, j)))
    if rope_half:
        for tab in rope_tabs:
            args.append(tab)
            in_specs.append(pl.BlockSpec((bm, LANES), lambda i, j, k: (i + i0, 0)))
    n_w = 2 if swiglu else 1
    scratch = [pltpu.VMEM((bm, bn), F32) for _ in range(n_w)] if nk > 1 else []
    est = (a_buffers * bm * bk * a.dtype.itemsize + (bm * bk * 2 if a.dtype != BF16 else 0)
           + 2 * n_w * bk * bn * w.dtype.itemsize + (n_w * bk * bn * 2 if w.dtype != BF16 else 0)
           + 2 * bm * bn * jnp.dtype(out_dtype).itemsize + (n_w + 2) * bm * bn * 4 + (4 << 20))
    return pl.pallas_call(
        functools.partial(_mm_kernel, nk=nk, swiglu=swiglu, norm_width=norm_width, rope_half=rope_half,
                          rope_every=rope_every, out_scale=out_scale, sub_n=sub_n),
        grid=(n_i, n // bn, nk),
        in_specs=in_specs,
        out_specs=pl.BlockSpec((bm, bn), lambda i, j, k: (i, j)),
        out_shape=jax.ShapeDtypeStruct((n_i * bm, n), out_dtype),
        scratch_shapes=scratch,
        compiler_params=_params(est, 3),
        name="matmul",
    )(*args)


def _dot_nt(q, k):
    return lax.dot_general(q, k, (((1,), (1,)), ((), ())), preferred_element_type=F32)


def _softmax_parts(s_parts):
    m = s_parts[0].max(axis=-1, keepdims=True)
    for s in s_parts[1:]:
        m = jnp.maximum(m, s.max(axis=-1, keepdims=True))
    e_parts = [jnp.exp2(s - m) for s in s_parts]
    denom = e_parts[0].sum(axis=-1, keepdims=True)
    for e in e_parts[1:]:
        denom = denom + e.sum(axis=-1, keepdims=True)
    return e_parts, denom


def _fill_kv(part_refs, kbf, vbf, row0, n_rows):
    k_ref, v_ref = part_refs
    kbf[row0:row0 + n_rows, :] = k_ref[...].astype(BF16)
    vbf[row0:row0 + n_rows, :] = v_ref[...].astype(BF16)


def _fill_mla(part_refs, kbf, vbf, row0, n_rows):
    kv_ref, kpe_ref = part_refs
    d = HEAD_DIM
    kpe = kpe_ref[...].astype(BF16)
    for slot in range(kv_ref.shape[1] // (2 * d)):
        kbf[row0:row0 + n_rows, 2 * slot * d:(2 * slot + 1) * d] = kv_ref[:, 2 * slot * d:(2 * slot + 1) * d]
        kbf[row0:row0 + n_rows, (2 * slot + 1) * d:(2 * slot + 2) * d] = kpe
        vbf[row0:row0 + n_rows, slot * d:(slot + 1) * d] = kv_ref[:, (2 * slot + 1) * d:(2 * slot + 2) * d]


def _attn_kernel(*refs, hpb, kv_share, n_maps, dk, dv, n_parts, refs_per_part, fill, band, lambda_init):
    refs = list(refs)
    if n_maps == 2:
        lam_ref, gain_ref = refs.pop(0), refs.pop(0)
    q_ref = refs.pop(0)
    part_refs = [tuple(refs.pop(0) for _ in range(refs_per_part)) for _ in range(n_parts)]
    bias_ref = refs.pop(0) if band else None
    o_ref = refs.pop(0)
    kbf, vbf = refs
    tq = q_ref.shape[0]
    part_rows = [p[0].shape[0] for p in part_refs]

    @pl.when(pl.program_id(2) == 0)
    def _():
        row0 = 0
        for p, n in zip(part_refs, part_rows):
            fill(p, kbf, vbf, row0, n)
            row0 += n

    if band:
        n_rows, band_rows = band
        wr = min(WIN_R, n_rows)
        r0 = pl.program_id(2) * (tq // GRID_W)
        assert part_rows[0] % GRID_W == 0
        start = part_rows[0] + jnp.clip(r0 - wr // 2, 0, n_rows - band_rows) * GRID_W
        key_rows = [pl.ds(0, part_rows[0]), pl.ds(pl.multiple_of(start, GRID_W), band_rows * GRID_W)]
    else:
        key_rows = [pl.ds(0, sum(part_rows))]

    if n_maps == 2:
        lam = (jnp.exp(jnp.sum(lam_ref[0:1, :] * lam_ref[1:2, :], axis=-1, keepdims=True))
               - jnp.exp(jnp.sum(lam_ref[2:3, :] * lam_ref[3:4, :], axis=-1, keepdims=True)) + lambda_init)

    for hs in range(hpb):
        slot = hs // kv_share
        maps = []
        for mp in range(n_maps):
            q = q_ref[:, (hs * n_maps + mp) * dk:(hs * n_maps + mp + 1) * dk]
            kcols = slice((slot * n_maps + mp) * dk, (slot * n_maps + mp + 1) * dk)
            s_parts = [_dot_nt(q, kbf[rows, kcols]) for rows in key_rows]
            if band:
                s_parts[-1] = s_parts[-1] + bias_ref[hs]
            maps.append(_softmax_parts(s_parts))
        vcols = slice(slot * dv, (slot + 1) * dv)
        if n_maps == 1:
            (e_parts, denom), = maps
            wgts = [e.astype(BF16) for e in e_parts]
        else:
            (e0, l0), (e1, l1) = maps
            ratio = lam * l0 / l1
            wgts = [(a - b * ratio).astype(BF16) for a, b in zip(e0, e1)]
            denom = l0
        o = jnp.dot(wgts[0], vbf[key_rows[0], vcols], preferred_element_type=F32)
        for wgt, rows in zip(wgts[1:], key_rows[1:]):
            o = o + jnp.dot(wgt, vbf[rows, vcols], preferred_element_type=F32)
        o = o / denom
        if n_maps == 2:
            o = _rms(o, gain_ref[...]) * (1.0 - lambda_init)
        o_ref[:, hs * dv:(hs + 1) * dv] = o.astype(o_ref.dtype)


def _drop_ref(kernel_fn, index):
    def wrapped(*refs):
        return kernel_fn(*refs[:index], *refs[index + 1:])
    return wrapped


def _attention_call(grid, operands, *, t_rows, heads, tq, q_tile_of, key_rows, cfg, name, into=None):
    hpb, dv, n_maps, dk = cfg["hpb"], cfg["dv"], cfg["n_maps"], cfg["dk"]
    slots = hpb // cfg["kv_share"]
    in_specs = [pl.BlockSpec(blk, imap) for _, blk, imap in operands]
    args = [arr for arr, _, _ in operands]
    kernel_fn = functools.partial(_attn_kernel, **cfg)
    aliases = {}
    if into is not None:
        kernel_fn = _drop_ref(kernel_fn, len(args))
        aliases = {len(args): 0}
        in_specs.append(pl.BlockSpec(memory_space=pl.ANY))
        args.append(into)
    est = (hpb * n_maps * tq * key_rows * 10 + 3 * key_rows * slots * (n_maps * dk + dv) * 4
           + 4 * tq * hpb * (n_maps * dk + dv) * 2 + (8 << 20))
    return pl.pallas_call(
        kernel_fn,
        grid=grid,
        in_specs=in_specs,
        out_specs=pl.BlockSpec((tq, hpb * dv), lambda b, g, i: (q_tile_of(b, i), g)),
        out_shape=jax.ShapeDtypeStruct((t_rows, heads * dv), BF16),
        scratch_shapes=[pltpu.VMEM((key_rows, slots * n_maps * dk), BF16),
                        pltpu.VMEM((key_rows, slots * dv), BF16)],
        input_output_aliases=aliases,
        compiler_params=_params(est, 3),
        name=name,
    )(*args)


def _rope_tables(n_ctx_rows, dec_batch, dec_seq, rot_dim):
    t = jnp.arange(dec_seq)
    per_axis = rot_dim // 2
    inv_freq = ROPE_BASE ** (-jnp.arange(0, per_axis, 2, dtype=F32) / per_axis)
    row = (t // GRID_W).astype(F32)[:, None] * inv_freq
    col = (t % GRID_W).astype(F32)[:, None] * inv_freq
    ang = jnp.concatenate([row, col], axis=-1)
    cos, sin = jnp.cos(ang), jnp.sin(ang)
    pad = LANES - rot_dim
    ones = jnp.ones((dec_seq, pad), F32)
    zeros_half = jnp.zeros_like(sin)
    zeros_pad = jnp.zeros((dec_seq, pad), F32)
    if pad == 0:
        tabs = [jnp.concatenate([cos, cos], -1), jnp.concatenate([-sin, sin], -1)]
    else:
        tabs = [jnp.concatenate([cos, cos, ones], -1),
                jnp.concatenate([-sin, zeros_half, zeros_pad], -1),
                jnp.concatenate([zeros_half, sin, zeros_pad], -1)]
    out = []
    for i, tab in enumerate(tabs):
        ctx = jnp.ones((n_ctx_rows, LANES), F32) if i == 0 else jnp.zeros((n_ctx_rows, LANES), F32)
        out.append(jnp.concatenate([ctx, jnp.tile(tab, (dec_batch, 1))], axis=0))
    return out


def _na_band_plan(n_rows, tq):
    wr = min(WIN_R, n_rows)
    rt = tq // GRID_W
    rows_per_lane_tile = LANES // GRID_W
    band_rows = min(n_rows, -(-(wr + rt - 1) // rows_per_lane_tile) * rows_per_lane_tile)
    cases, case_of_tile = [], []
    for qt in range(n_rows // rt):
        r0 = qt * rt
        bs = int(np.clip(r0 - wr // 2, 0, n_rows - band_rows))
        r = r0 + np.arange(rt)[:, None]
        kr = bs + np.arange(band_rows)[None, :]
        rs = np.clip(r - wr // 2, 0, n_rows - wr)
        valid = (kr >= rs) & (kr < rs + wr)
        ridx = np.clip(kr - r + WIN_R - 1, 0, 2 * WIN_R - 2)
        key = (valid.tobytes(), ridx.tobytes())
        for ci, (k0, _, _) in enumerate(cases):
            if k0 == key:
                break
        else:
            ci = len(cases)
            cases.append((key, ridx, valid))
        case_of_tile.append(ci)
    if any(b < a for a, b in zip(case_of_tile, case_of_tile[1:])) or \
            sorted(set(case_of_tile)) != list(range(len(cases))):
        raise NotImplementedError("neighbourhood tile geometries are not monotone in the tile index")
    thresholds = [case_of_tile.index(ci) for ci in range(1, len(cases))]
    ridx = np.stack([c[1] for c in cases])
    valid = np.stack([c[2] for c in cases])
    return band_rows, ridx, valid, thresholds


def _na_bias(rpb, ridx, valid):
    col = jnp.arange(GRID_W)
    col_start = jnp.clip(col - WIN_C // 2, 0, GRID_W - WIN_C)
    in_win = (col[None, :] >= col_start[:, None]) & (col[None, :] < col_start[:, None] + WIN_C)
    col_off = jnp.clip(col[None, :] - col[:, None] + WIN_C - 1, 0, 2 * WIN_C - 2)
    col_bias = jnp.where(in_win[None, None], rpb[:, :, col_off].astype(F32) * LOG2_E, NEG_INF)
    masked = jnp.full_like(col_bias[:, :1], NEG_INF)
    col_bias_q = jnp.transpose(jnp.concatenate([col_bias, masked], axis=1), (0, 2, 1, 3))
    n_case, rt, band_rows = ridx.shape
    sel = np.where(valid, ridx, 2 * WIN_R - 1)
    heads = rpb.shape[0]
    cases = []
    for ci in range(n_case):
        rows = [jnp.take(col_bias_q, jnp.asarray(sel[ci, qr]), axis=2).reshape(heads, GRID_W, band_rows * GRID_W)
                for qr in range(rt)]
        cases.append(jnp.concatenate(rows, axis=1))
    return jnp.stack(cases, axis=1)


def kernel(x_prompt, x_sample, cache_gqa_k, cache_gqa_v, cache_mla_ckv, cache_mla_kpe, cache_diff_k, cache_diff_v, cache_na_k, cache_na_v, c, c_ctx, ada_w, ada_b, norm_mix_pre, norm_mix_post, norm_ffn_pre, norm_ffn_post, ffn_w_gu, ffn_w_down, gqa_w_qkv, gqa_q_norm, gqa_k_norm, gqa_w_o, mla_w_in, mla_q_norm, mla_kv_norm, mla_w_q_up, mla_w_kv_up, mla_w_o, diff_w_qkv, diff_lam_q1, diff_lam_k1, diff_lam_q2, diff_lam_k2, diff_subln, diff_w_o, na_w_qkv, na_rpb, na_w_o):
    nb, seq, d = x_prompt.shape
    db, dseq, _ = x_sample.shape
    past = cache_gqa_k.shape[2]
    depth = ada_w.shape[0]
    tp, ts = nb * seq, db * dseq
    t = tp + ts
    hd = HEAD_DIM
    assert db + 1 <= MOD_ROWS and dseq % GRID_W == 0
    assert tp % dseq == 0 and t % past == 0 and tp % past == 0

    tr = _pick(math.gcd(seq, dseq), 256, 8)
    n_ctx_tiles, tiles_per_dec = tp // tr, dseq // tr

    def seg_of_tile_at(layer):
        return lambda i: layer * MOD_ROWS + jnp.where(i < n_ctx_tiles, 0, 1 + (i - n_ctx_tiles) // tiles_per_dec)

    cond = jnp.concatenate([c_ctx[None, :], c, jnp.zeros((MOD_ROWS - 1 - db, d), F32)], axis=0)
    mods = _adaln(cond, ada_w, ada_b).reshape(depth * MOD_ROWS, 1, 6 * d)

    x = (x_prompt.reshape(tp, d), x_sample.reshape(ts, d))
    o_buf = [None]

    rope128 = _rope_tables(tp, db, dseq, hd)
    rope64 = _rope_tables(tp, db, dseq, B_ROPE_DIM)
    lat_blk0 = tp // dseq
    tq_dense = _pick(dseq, 512, GRID_W)

    def vec(v):
        return v.reshape(1, -1).astype(F32)

    def ctx_out(arr, shape):
        return arr[:tp].reshape((nb, seq) + shape)

    def attention(q, ctx_part, lat_parts, *, heads, kv_share, n_maps, dk, dv, hpb_ctx, hpb_lat, tq_lat, name,
                  fill=_fill_kv, head_operands=(), bias=None, band=None, lambda_init=0.0):
        def part_operands(part, hpb):
            slots = hpb // kv_share
            ops = []
            for arr, rows, blk_of, cols in part:
                if cols is None:
                    ops.append((arr, (rows, LANES), lambda b, g, i, blk_of=blk_of: (blk_of(b), 0)))
                else:
                    ops.append((arr, (rows, slots * cols), lambda b, g, i, blk_of=blk_of: (blk_of(b), g)))
            return ops

        def cfg(hpb, n_parts, band_cfg):
            return dict(hpb=hpb, kv_share=kv_share, n_maps=n_maps, dk=dk, dv=dv, n_parts=n_parts,
                        refs_per_part=len(ctx_part), fill=fill, band=band_cfg, lambda_init=lambda_init)

        qw = n_maps * dk
        hpb_c = min(hpb_ctx, heads)
        ops = [(a, blk, (lambda b, g, i: (0, 0))) for a, blk in head_operands]
        ops.append((q, (seq, hpb_c * qw), lambda b, g, i: (b, g)))
        ops += part_operands(ctx_part, hpb_c)
        buf = o_buf[0]
        if buf is None or buf.shape != (t, heads * dv):
            buf = jnp.zeros((t, heads * dv), BF16)
        o = _attention_call((nb, heads // hpb_c, 1), ops, t_rows=t, heads=heads, tq=seq,
                            q_tile_of=lambda b, i: b, key_rows=seq, cfg=cfg(hpb_c, 1, None), name=name + "_ctx",
                            into=buf)
        hpb_l = min(hpb_lat, heads)
        n_qt, ctx_tiles = dseq // tq_lat, tp // tq_lat

        def q_tile_of(b, i):
            return ctx_tiles + b * n_qt + i

        ops = [(a, blk, (lambda b, g, i: (0, 0))) for a, blk in head_operands]
        ops.append((q, (tq_lat, hpb_l * qw), lambda b, g, i: (q_tile_of(b, i), g)))
        for part in lat_parts:
            ops += part_operands(part, hpb_l)
        if bias is not None:
            bias_arr, case_of = bias
            ops.append((bias_arr, (hpb_l, None) + bias_arr.shape[2:], lambda b, g, i: (g, case_of(i), 0, 0)))
        return _attention_call((db, heads // hpb_l, n_qt), ops, t_rows=t, heads=heads, tq=tq_lat,
                               q_tile_of=q_tile_of, key_rows=past + dseq, cfg=cfg(hpb_l, len(lat_parts), band),
                               name=name + "_lat", into=o)

    bm_kv = _pick(math.gcd(tp, ts), 1024, 8)

    def kv_project(w, **epilogue):
        ctx = _matmul(h, w, F32, bm=bm_kv, row_blocks=(0, tp // bm_kv), **epilogue)
        lat = _matmul(h, w, BF16, bm=bm_kv, row_blocks=(tp // bm_kv, ts // bm_kv), **epilogue)
        return ctx, lat

    def kv_parts(k, v, ck, cv, cols_k, cols_v):
        ctx = [(k[0], seq, lambda b: b, cols_k), (v[0], seq, lambda b: b, cols_v)]
        lat = [[(ck, past, lambda b: b, cols_k), (cv, past, lambda b: b, cols_v)],
               [(k[1], dseq, lambda b: b, cols_k), (v[1], dseq, lambda b: b, cols_v)]]
        return ctx, lat

    outs = {k: [] for k in ("gqa_k", "gqa_v", "mla_ckv", "mla_kpe", "diff_k", "diff_v", "na_k", "na_v")}

    ffn_down_bf16 = ffn_w_down.astype(BF16)

    (h,) = _rows(x, None, mods, tr, pre=(vec(norm_mix_pre[0]), seg_of_tile_at(0), 0, 1))

    for layer in range(depth):
        kind, j = layer % N_MIXERS, layer // N_MIXERS
        seg = seg_of_tile_at(layer)

        if kind == 0:
            w = gqa_w_qkv[j].astype(BF16)
            n_q = gqa_w_o.shape[1]
            n_kv = (w.shape[1] - n_q) // 2
            heads, kv_heads = n_q // hd, n_kv // hd
            group = heads // kv_heads
            q_scale = hd ** -0.5 * LOG2_E
            q = _matmul(h, w, BF16, w_cols=(0, n_q), gain=jnp.tile(vec(gqa_q_norm[j]), (1, heads)), norm_width=hd,
                        rope_tabs=rope128, rope_half=hd // 2, out_scale=q_scale)
            k = kv_project(w, w_cols=(n_q, n_kv), gain=jnp.tile(vec(gqa_k_norm[j]), (1, kv_heads)),
                           norm_width=hd, rope_tabs=rope128, rope_half=hd // 2)
            v = kv_project(w, w_cols=(n_q + n_kv, n_kv))
            outs["gqa_k"].append(ctx_out(k[0], (kv_heads, hd)))
            outs["gqa_v"].append(ctx_out(v[0], (kv_heads, hd)))
            ck = cache_gqa_k[:, j].reshape(db * past, n_kv)
            cv = cache_gqa_v[:, j].reshape(db * past, n_kv)
            ctx_part, lat_parts = kv_parts(k, v, ck, cv, hd, hd)
            o = attention(q, ctx_part, lat_parts, heads=heads, kv_share=group, n_maps=1, dk=hd, dv=hd,
                          hpb_ctx=4 * group, hpb_lat=2 * group, tq_lat=tq_dense, name="attn_gqa")
            w_o = gqa_w_o[j]

        elif kind == 1:
            q_rank, kv_rank = mla_q_norm.shape[1], mla_kv_norm.shape[1]
            w_in = mla_w_in[j].astype(BF16)
            rope_dim = w_in.shape[1] - q_rank - kv_rank
            w_kpe = jnp.pad(w_in[:, q_rank + kv_rank:], ((0, 0), (0, LANES - rope_dim)))
            heads = mla_w_o.shape[1] // hd
            q_scale = (hd + rope_dim) ** -0.5 * LOG2_E
            cq = _matmul(h, w_in, BF16, w_cols=(0, q_rank), gain=vec(mla_q_norm[j]), norm_width=q_rank)
            ckv = _matmul(h, w_in, F32, w_cols=(q_rank, kv_rank), gain=vec(mla_kv_norm[j]), norm_width=kv_rank)
            kpe = _matmul(h, w_kpe, F32, rope_tabs=rope64, rope_half=rope_dim // 2)
            outs["mla_ckv"].append(ctx_out(ckv, (kv_rank,)))
            outs["mla_kpe"].append(ctx_out(kpe[:, :rope_dim], (rope_dim,)))
            w_q = jnp.pad(mla_w_q_up[j].astype(BF16).reshape(q_rank, heads, hd + rope_dim),
                          ((0, 0), (0, 0), (0, LANES - rope_dim))).reshape(q_rank, heads * 2 * hd)
            q = _matmul(cq, w_q, BF16, rope_tabs=rope64, rope_half=rope_dim // 2, rope_every=2, out_scale=q_scale)
            ckv_all = jnp.concatenate([ckv, cache_mla_ckv[:, j].reshape(db * past, kv_rank)], axis=0)
            kv = _matmul(ckv_all, mla_w_kv_up[j].astype(BF16), BF16)
            ckpe = jnp.pad(cache_mla_kpe[:, j].reshape(db * past, rope_dim), ((0, 0), (0, LANES - rope_dim)))
            cache_blk0 = t // past
            ctx_part = [(kv, seq, lambda b: b, 2 * hd), (kpe, seq, lambda b: b, None)]
            lat_parts = [[(kv, past, lambda b: cache_blk0 + b, 2 * hd), (ckpe, past, lambda b: b, None)],
                         [(kv, dseq, lambda b: lat_blk0 + b, 2 * hd), (kpe, dseq, lambda b: lat_blk0 + b, None)]]
            o = attention(q, ctx_part, lat_parts, heads=heads, kv_share=1, n_maps=1, dk=2 * hd, dv=hd,
                          hpb_ctx=16, hpb_lat=4, tq_lat=tq_dense, name="attn_mla", fill=_fill_mla)
            w_o = mla_w_o[j]

        elif kind == 2:
            lambda_init = 0.8 - 0.6 * math.exp(-0.3 * layer)
            w = diff_w_qkv[j].astype(BF16)
            n_w = w.shape[1] // 3
            heads = n_w // (2 * hd)
            q_scale = hd ** -0.5 * LOG2_E
            q = _matmul(h, w, BF16, w_cols=(0, n_w), rope_tabs=rope128, rope_half=hd // 2, out_scale=q_scale)
            k = kv_project(w, w_cols=(n_w, n_w), rope_tabs=rope128, rope_half=hd // 2)
            v = kv_project(w, w_cols=(2 * n_w, n_w))
            outs["diff_k"].append(ctx_out(k[0], (heads, 2, hd)))
            outs["diff_v"].append(ctx_out(v[0], (heads, 2 * hd)))
            ck = cache_diff_k[:, j].reshape(db * past, n_w)
            cv = cache_diff_v[:, j].reshape(db * past, n_w)
            lam = jnp.stack([diff_lam_q1[j], diff_lam_k1[j], diff_lam_q2[j], diff_lam_k2[j]]).astype(F32)
            ctx_part, lat_parts = kv_parts(k, v, ck, cv, 2 * hd, 2 * hd)
            o = attention(q, ctx_part, lat_parts, heads=heads, kv_share=1, n_maps=2, dk=hd, dv=2 * hd,
                          hpb_ctx=8, hpb_lat=4, tq_lat=tq_dense, name="attn_diff",
                          head_operands=[(lam, (4, hd)), (vec(diff_subln[j]), (1, 2 * hd))], lambda_init=lambda_init)
            w_o = diff_w_o[j]

        else:
            w = na_w_qkv[j].astype(BF16)
            n_w = w.shape[1] // 3
            heads = n_w // hd
            q_scale = hd ** -0.5 * LOG2_E
            q = _matmul(h, w, BF16, w_cols=(0, n_w), out_scale=q_scale)
            k = kv_project(w, w_cols=(n_w, n_w))
            v = kv_project(w, w_cols=(2 * n_w, n_w))
            outs["na_k"].append(ctx_out(k[0], (heads, hd)))
            outs["na_v"].append(ctx_out(v[0], (heads, hd)))
            ck = cache_na_k[:, j].reshape(db * past, n_w)
            cv = cache_na_v[:, j].reshape(db * past, n_w)
            n_rows = dseq // GRID_W
            tq_na = _pick(dseq, 256, GRID_W)
            band_rows, ridx, valid, thresholds = _na_band_plan(n_rows, tq_na)
            bias = _na_bias(na_rpb[j], ridx, valid)

            def case_of(i):
                return sum((i >= th).astype(jnp.int32) for th in thresholds) if thresholds else 0

            ctx_part, lat_parts = kv_parts(k, v, ck, cv, hd, hd)
            o = attention(q, ctx_part, lat_parts, heads=heads, kv_share=1, n_maps=1, dk=hd, dv=hd,
                          hpb_ctx=16, hpb_lat=8, tq_lat=tq_na, name="attn_na", bias=(bias, case_of),
                          band=(n_rows, band_rows))
            w_o = na_w_o[j]

        y = _matmul(o, w_o.astype(BF16), F32)
        o_buf[0] = o
        x, h = _rows(x, y, mods, tr, post=(vec(norm_mix_post[layer]), seg, 2),
                     pre=(vec(norm_ffn_pre[layer]), seg, 3, 4))

        act = _matmul(h, ffn_w_gu, BF16, layer=layer, swiglu=True, bm=2048, bn=256, single_buffer_a=True)
        y = _matmul(act, ffn_down_bf16, F32, layer=layer, bm=512, bn=512)
        post = (vec(norm_ffn_post[layer]), seg, 5)
        if layer + 1 < depth:
            x, h = _rows(x, y, mods, tr, post=post,
                         pre=(vec(norm_mix_pre[layer + 1]), seg_of_tile_at(layer + 1), 0, 1))
        else:
            (y_prompt,) = _rows(x, y, mods, tr, post=post, tile0=0, n_tiles=n_ctx_tiles)
            (y_sample,) = _rows(x, y, mods, tr, post=post, tile0=n_ctx_tiles, n_tiles=ts // tr)

    return (y_prompt.reshape(nb, seq, d), y_sample.reshape(db, dseq, d),
            jnp.stack(outs["gqa_k"], axis=1), jnp.stack(outs["gqa_v"], axis=1),
            jnp.stack(outs["mla_ckv"], axis=1), jnp.stack(outs["mla_kpe"], axis=1),
            jnp.stack(outs["diff_k"], axis=1), jnp.stack(outs["diff_v"], axis=1),
            jnp.stack(outs["na_k"], axis=1), jnp.stack(outs["na_v"], axis=1))
```
